```python
import math
import jax, jax.numpy as jnp
from jax import lax
import numpy as np

D_MODEL = 1024
BATCH = 8
SEQ = 4096
DEPTH = 2

N_MEM = 256
EPS = 1e-6
N_EVEN = (DEPTH + 1) // 2
N_ODD = DEPTH // 2
DN_HEADS = 4
DN_DK = D_MODEL // 8
DN_DV = D_MODEL // 8
DN_W = DN_HEADS * DN_DK
DN_CONV = 4
DN_CHUNK = 64
SB_HEADS = 4
SB_DH = D_MODEL // 8
SB_W = SB_HEADS * SB_DH
SB_BLOCK = 128
AB_IN = 3 * DN_W + DN_HEADS * DN_DV + 2 * DN_HEADS + 3 * SB_W
GLA_HEADS = 4
GLA_DK = D_MODEL // 8
GLA_DV = D_MODEL // 4
GLA_RANK = 16
GLA_TAU = 16.0
GLA_CHUNK = 64
C_IN = 2 * GLA_HEADS * GLA_DK + 2 * GLA_HEADS * GLA_DV + GLA_RANK
XA_HEADS = 4
XA_DH = D_MODEL // XA_HEADS
D_FF = ((8 * D_MODEL // 3 + 255) // 256) * 256
FFN_CONV = 3

kernel_name = "hybrid_deltanet_stickbreak_gla_convffn"


def rmsnorm(x, g):
    xf = x.astype(jnp.float32)
    y = xf * lax.rsqrt(jnp.mean(xf * xf, axis=-1, keepdims=True) + EPS)
    return (y * g.astype(jnp.float32)).astype(x.dtype)


def l2norm(x):
    return x * lax.rsqrt(jnp.sum(x * x, axis=-1, keepdims=True) + EPS)


def _split(p, sizes):
    idx, acc = [], 0
    for s in sizes[:-1]:
        acc += s
        idx.append(acc)
    return jnp.split(p, idx, axis=-1)


def causal_dwconv(x, w):
    width, ch = w.shape
    return lax.conv_general_dilated(
        x, w.astype(x.dtype)[:, None, :], window_strides=(1,),
        padding=[(width - 1, 0)], dimension_numbers=('NWC', 'WIO', 'NWC'),
        feature_group_count=ch)


def _chunk(t, c):
    b, s, h, d = t.shape
    return t.reshape(b, s // c, c, h, d).transpose(0, 3, 1, 2, 4)


def _chunk_h(t, c):
    b, s, h = t.shape
    return t.reshape(b, s // c, c, h).transpose(0, 3, 1, 2)


def gated_deltanet(q, k, v, a, b, a_log, dt_bias):
    f32 = jnp.float32
    bsz, s, h, dk = q.shape
    dv = v.shape[-1]
    c = DN_CHUNK
    q = l2norm(q.astype(f32)) * dk ** -0.5
    k = l2norm(k.astype(f32))
    v = v.astype(f32)
    beta = jax.nn.sigmoid(b.astype(f32))
    g = -jnp.exp(a_log.astype(f32)) * jax.nn.softplus(a.astype(f32) + dt_bias.astype(f32))
    q, k, v = _chunk(q, c), _chunk(k, c), _chunk(v, c)
    beta, g = _chunk_h(beta, c), _chunk_h(g, c)
    gc = jnp.cumsum(g, axis=-1)
    causal = jnp.tril(jnp.ones((c, c), bool))
    strict = jnp.tril(jnp.ones((c, c), bool), -1)
    diff = gc[..., :, None] - gc[..., None, :]
    decay = jnp.where(causal, jnp.exp(jnp.where(causal, diff, 0.0)), 0.0)
    kb = k * beta[..., None]
    a_mat = jnp.where(strict, jnp.einsum('bhnid,bhnjd->bhnij', kb, k) * decay, 0.0)
    m_mat = a_mat + jnp.eye(c, dtype=f32)
    u = lax.linalg.triangular_solve(m_mat, v * beta[..., None], left_side=True,
                                    lower=True, unit_diagonal=True)
    w = lax.linalg.triangular_solve(m_mat, kb * jnp.exp(gc)[..., None], left_side=True,
                                    lower=True, unit_diagonal=True)
    qk = jnp.einsum('bhnid,bhnjd->bhnij', q, k) * decay
    qg = q * jnp.exp(gc)[..., None]
    glast = gc[..., -1]
    kdec = k * jnp.exp(glast[..., None] - gc)[..., None]

    def step(state, inp):
        u_n, w_n, qk_n, qg_n, kdec_n, gl_n = inp
        delta = u_n - jnp.einsum('bhck,bhkv->bhcv', w_n, state)
        o = (jnp.einsum('bhck,bhkv->bhcv', qg_n, state)
             + jnp.einsum('bhij,bhjv->bhiv', qk_n, delta))
        state = (jnp.exp(gl_n)[..., None, None] * state
                 + jnp.einsum('bhck,bhcv->bhkv', kdec_n, delta))
        return state, o

    s0 = jnp.zeros((bsz, h, dk, dv), f32)
    xs = tuple(jnp.moveaxis(t, 2, 0) for t in (u, w, qk, qg, kdec, glast))
    _, o = lax.scan(step, s0, xs)
    return o.transpose(1, 0, 3, 2, 4).reshape(bsz, s, h, dv)


def stick_breaking(q, k, v):
    s = q.shape[2]
    scale = q.shape[-1] ** -0.5
    outs = []
    for blk in range(s // SB_BLOCK):
        q0 = blk * SB_BLOCK
        end = q0 + SB_BLOCK
        z = jnp.einsum('bhqd,bhkd->bhqk', q[:, :, q0:end], k[:, :, :end]) * scale
        tpos = q0 + jnp.arange(SB_BLOCK)
        spos = jnp.arange(end)
        mask = spos[None, :] < tpos[:, None]
        l = jnp.where(mask, jax.nn.log_sigmoid(-z), 0.0)
        log_a = z + lax.cumsum(l, axis=3, reverse=True)
        att = jnp.exp(jnp.where(mask, log_a, -jnp.inf))
        outs.append(jnp.einsum('bhqk,bhkd->bhqd', att, v[:, :, :end]))
    return jnp.concatenate(outs, axis=2)


def mixer_deltanet_stickbreak(h, w_in, conv_w, a_log, dt_bias, dn_norm_g, sb_g_q, sb_g_k, w_out):
    f32 = jnp.float32
    bsz, s, _ = h.shape
    p = h @ w_in
    qkv_a, z_a, a_a, b_a, q_b, k_b, v_b = _split(
        p, [3 * DN_W, DN_HEADS * DN_DV, DN_HEADS, DN_HEADS, SB_W, SB_W, SB_W])
    qkv_a = jax.nn.silu(causal_dwconv(qkv_a, conv_w))
    q_a, k_a, v_a = _split(qkv_a, [DN_W, DN_W, DN_HEADS * DN_DV])
    o_a = gated_deltanet(q_a.reshape(bsz, s, DN_HEADS, DN_DK),
                         k_a.reshape(bsz, s, DN_HEADS, DN_DK),
                         v_a.reshape(bsz, s, DN_HEADS, DN_DV), a_a, b_a, a_log, dt_bias)
    o_a = rmsnorm(o_a, dn_norm_g) * jax.nn.silu(z_a.reshape(bsz, s, DN_HEADS, DN_DV).astype(f32))
    q_b = rmsnorm(q_b.reshape(bsz, s, SB_HEADS, SB_DH), sb_g_q).astype(f32).transpose(0, 2, 1, 3)
    k_b = rmsnorm(k_b.reshape(bsz, s, SB_HEADS, SB_DH), sb_g_k).astype(f32).transpose(0, 2, 1, 3)
    v_b = v_b.reshape(bsz, s, SB_HEADS, SB_DH).astype(f32).transpose(0, 2, 1, 3)
    o_b = stick_breaking(q_b, k_b, v_b).transpose(0, 2, 1, 3)
    o = jnp.concatenate([o_a.reshape(bsz, s, -1), o_b.reshape(bsz, s, -1)], axis=-1)
    return o.astype(h.dtype) @ w_out


def gla_attention(q, k, v, gk):
    f32 = jnp.float32
    bsz, s, h, dk = q.shape
    dv = v.shape[-1]
    c = GLA_CHUNK
    q = _chunk(q.astype(f32) * dk ** -0.5, c)
    k = _chunk(k.astype(f32), c)
    v = _chunk(v.astype(f32), c)
    bc = jnp.cumsum(_chunk(gk.astype(f32), c), axis=3)
    bmid = bc[..., c // 2:c // 2 + 1, :]
    causal = jnp.tril(jnp.ones((c, c), bool))
    att = jnp.einsum('bhnid,bhnjd->bhnij', q * jnp.exp(bc - bmid), k * jnp.exp(bmid - bc))
    o_intra = jnp.einsum('bhnij,bhnjv->bhniv', jnp.where(causal, att, 0.0), v)
    qg = q * jnp.exp(bc)
    blast = bc[..., -1, :]
    kdec = k * jnp.exp(blast[..., None, :] - bc)

    def step(state, inp):
        qg_n, kdec_n, v_n, bl_n = inp
        o = jnp.einsum('bhck,bhkv->bhcv', qg_n, state)
        state = jnp.exp(bl_n)[..., None] * state + jnp.einsum('bhck,bhcv->bhkv', kdec_n, v_n)
        return state, o

    s0 = jnp.zeros((bsz, h, dk, dv), f32)
    xs = tuple(jnp.moveaxis(t, 2, 0) for t in (qg, kdec, v, blast))
    _, o_inter = lax.scan(step, s0, xs)
    o = o_intra + jnp.moveaxis(o_inter, 0, 2)
    return o.transpose(0, 2, 3, 1, 4).reshape(bsz, s, h, dv)


def mixer_gla(h, w_in, w_gk, b_gk, norm_g, w_out):
    f32 = jnp.float32
    bsz, s, _ = h.shape
    p = h @ w_in
    q, k, v, r, lr = _split(p, [GLA_HEADS * GLA_DK, GLA_HEADS * GLA_DK,
                                GLA_HEADS * GLA_DV, GLA_HEADS * GLA_DV, GLA_RANK])
    gk = jax.nn.log_sigmoid((lr @ w_gk + b_gk).astype(f32)) / GLA_TAU
    o = gla_attention(q.reshape(bsz, s, GLA_HEADS, GLA_DK), k.reshape(bsz, s, GLA_HEADS, GLA_DK),
                      v.reshape(bsz, s, GLA_HEADS, GLA_DV), gk.reshape(bsz, s, GLA_HEADS, GLA_DK))
    o = rmsnorm(o, norm_g) * jax.nn.silu(r.reshape(bsz, s, GLA_HEADS, GLA_DV).astype(f32))
    return o.reshape(bsz, s, -1).astype(h.dtype) @ w_out


def mem_cross_attention(h, m, w_q, w_kv, w_o, g_q, g_k):
    f32 = jnp.float32
    bsz, s, d = h.shape
    q = rmsnorm((h @ w_q).reshape(bsz, s, XA_HEADS, XA_DH), g_q)
    kv = (m @ w_kv).reshape(bsz, m.shape[1], 2, XA_HEADS, XA_DH)
    k = rmsnorm(kv[:, :, 0], g_k)
    v = kv[:, :, 1]
    sc = jnp.einsum('bshd,bmhd->bhsm', q.astype(f32), k.astype(f32)) * XA_DH ** -0.5
    pr = jax.nn.softmax(sc, axis=-1)
    o = jnp.einsum('bhsm,bmhd->bshd', pr, v.astype(f32)).reshape(bsz, s, d)
    return o.astype(h.dtype) @ w_o


def conv_ffn(h, w_up, conv_w, conv_b, w_down):
    u = causal_dwconv(h @ w_up, conv_w) + conv_b.astype(h.dtype)
    g, val = jnp.split(u, 2, axis=-1)
    return (jax.nn.silu(g) * val) @ w_down


def setup_inputs(seed: int = 0) -> dict:
    key = jax.random.key(seed)
    f32 = jnp.float32
    counter = [0]

    def nk():
        counter[0] += 1
        return jax.random.fold_in(key, counter[0])

    def nrm(shape, scale):
        return jax.random.normal(nk(), shape, f32) * scale

    def gain(shape):
        return 1.0 + 0.01 * jax.random.normal(nk(), shape, f32)

    d = D_MODEL
    dt = jnp.exp(jax.random.uniform(nk(), (N_EVEN, DN_HEADS), f32, math.log(1e-3), math.log(1e-1)))
    return {
        "x": nrm((BATCH, SEQ, d), 1.0),
        "mem": nrm((BATCH, N_MEM, d), 1.0),
        "norm_mix": gain((DEPTH, d)),
        "norm_xa": gain((DEPTH, d)),
        "norm_mem": gain((DEPTH, d)),
        "norm_ffn": gain((DEPTH, d)),
        "xa_w_q": nrm((DEPTH, d, d), d ** -0.5),
        "xa_w_kv": nrm((DEPTH, d, 2 * d), d ** -0.5),
        "xa_w_o": nrm((DEPTH, d, d), d ** -0.5),
        "xa_g_q": gain((DEPTH, XA_DH)),
        "xa_g_k": gain((DEPTH, XA_DH)),
        "ffn_w_up": nrm((DEPTH, d, 2 * D_FF), d ** -0.5),
        "ffn_conv_w": nrm((DEPTH, FFN_CONV, 2 * D_FF), FFN_CONV ** -0.5),
        "ffn_conv_b": nrm((DEPTH, 2 * D_FF), 0.01),
        "ffn_w_down": nrm((DEPTH, D_FF, d), D_FF ** -0.5),
        "ab_w_in": nrm((N_EVEN, d, AB_IN), d ** -0.5),
        "ab_conv_w": nrm((N_EVEN, DN_CONV, 3 * DN_W), DN_CONV ** -0.5),
        "dn_a_log": jnp.log(jax.random.uniform(nk(), (N_EVEN, DN_HEADS), f32, 1.0, 16.0)),
        "dn_dt_bias": dt + jnp.log(-jnp.expm1(-dt)),
        "dn_norm_g": gain((N_EVEN, DN_DV)),
        "sb_g_q": gain((N_EVEN, SB_DH)),
        "sb_g_k": gain((N_EVEN, SB_DH)),
        "ab_w_out": nrm((N_EVEN, DN_HEADS * DN_DV + SB_W, d), d ** -0.5),
        "gla_w_in": nrm((N_ODD, d, C_IN), d ** -0.5),
        "gla_w_gk": nrm((N_ODD, GLA_RANK, GLA_HEADS * GLA_DK), GLA_RANK ** -0.5),
        "gla_b_gk": nrm((N_ODD, GLA_HEADS * GLA_DK), 0.01),
        "gla_norm_g": gain((N_ODD, GLA_DV)),
        "gla_w_out": nrm((N_ODD, GLA_HEADS * GLA_DV, d), d ** -0.5),
    }


def reference(x, mem, norm_mix, norm_xa, norm_mem, norm_ffn,
              xa_w_q, xa_w_kv, xa_w_o, xa_g_q, xa_g_k,
              ffn_w_up, ffn_conv_w, ffn_conv_b, ffn_w_down,
              ab_w_in, ab_conv_w, dn_a_log, dn_dt_bias, dn_norm_g, sb_g_q, sb_g_k, ab_w_out,
              gla_w_in, gla_w_gk, gla_b_gk, gla_norm_g, gla_w_out):
    for layer in range(DEPTH):
        i = layer // 2
        h = rmsnorm(x, norm_mix[layer])
        if layer % 2 == 0:
            x = x + mixer_deltanet_stickbreak(h, ab_w_in[i], ab_conv_w[i], dn_a_log[i], dn_dt_bias[i],
                                              dn_norm_g[i], sb_g_q[i], sb_g_k[i], ab_w_out[i])
        else:
            x = x + mixer_gla(h, gla_w_in[i], gla_w_gk[i], gla_b_gk[i], gla_norm_g[i], gla_w_out[i])
        x = x + mem_cross_attention(rmsnorm(x, norm_xa[layer]), rmsnorm(mem, norm_mem[layer]),
                                    xa_w_q[layer], xa_w_kv[layer], xa_w_o[layer],
                                    xa_g_q[layer], xa_g_k[layer])
        x = x + conv_ffn(rmsnorm(x, norm_ffn[layer]), ffn_w_up[layer], ffn_conv_w[layer],
                         ffn_conv_b[layer], ffn_w_down[layer])
    return x
```

```python
import functools

import jax
import jax.numpy as jnp
from jax import lax
from jax.experimental import pallas as pl
from jax.experimental.pallas import tpu as pltpu

F32 = jnp.float32
BF16 = jnp.bfloat16
EPS = 1e-6

DN_HEADS = 4
SB_HEADS = 4
GLA_HEADS = 4
XA_HEADS = 4
GLA_TAU = 16.0
CHUNK = 64
LANES = 128
SUBLANES = 8
VMEM_LIMIT = 56 * 1024 * 1024


def _cparams(sem):
    return pltpu.CompilerParams(dimension_semantics=sem, vmem_limit_bytes=VMEM_LIMIT)


def _rms(x, g):
    ms = jnp.mean(x * x, axis=-1, keepdims=True)
    return x * lax.rsqrt(ms + EPS) * g


def _l2n(x):
    return x * lax.rsqrt(jnp.sum(x * x, axis=-1, keepdims=True) + EPS)


def _dot(a, b):
    return jnp.dot(a, b, preferred_element_type=F32)


def _dot_nt(a, b):
    return lax.dot_general(a, b, (((1,), (1,)), ((), ())), preferred_element_type=F32)


def _split2(x):
    hi = x.astype(BF16)
    lo = (x - hi.astype(F32)).astype(BF16)
    return hi, lo


def _split3(x):
    hi = x.astype(BF16)
    r = x - hi.astype(F32)
    mid = r.astype(BF16)
    lo = (r - mid.astype(F32)).astype(BF16)
    return hi, mid, lo


def _sigmoid(x):
    return 1.0 / (1.0 + jnp.exp(-x))


def _silu(x):
    return x * _sigmoid(x)


def _softplus(x):
    return jnp.maximum(x, 0.0) + jnp.log1p(jnp.exp(-jnp.abs(x)))


def _chunk_tri(n):
    r = lax.broadcasted_iota(jnp.int32, (n, n), 0)
    c = lax.broadcasted_iota(jnp.int32, (n, n), 1)
    same = (r // CHUNK) == (c // CHUNK)
    return jnp.where(same & (c <= r), 1.0, 0.0).astype(BF16)


def _proj_kernel(*refs, plan, n_aux):
    x_ref, g_ref, w_ref = refs[:3]
    aux = refs[3:3 + n_aux]
    outs = refs[3 + n_aux:]
    hn = _rms(x_ref[...], g_ref[...]).astype(BF16)
    ai = 0
    for (c0, width, mode, hd), o_ref in zip(plan, outs):
        if mode == "headnorm":
            gain = aux[ai][...]
            ai += 1
        elif mode == "gk":
            wgk_ref, bgk_ref = aux[ai], aux[ai + 1]
            ai += 2
        step = min(width, 512)
        for s0 in range(0, width, step):
            p = _dot(hn, w_ref[:, c0 + s0:c0 + s0 + step])
            if mode == "raw":
                o_ref[:, s0:s0 + step] = p.astype(o_ref.dtype)
            elif mode == "headnorm":
                for h0 in range(0, step, hd):
                    o_ref[:, s0 + h0:s0 + h0 + hd] = _rms(p[:, h0:h0 + hd], gain).astype(o_ref.dtype)
            else:
                y = _dot(p.astype(BF16), wgk_ref[...]) + bgk_ref[...]
                o_ref[...] = (jnp.minimum(y, 0.0) - jnp.log1p(jnp.exp(-jnp.abs(y)))) / GLA_TAU


def _proj(x, g, w, aux, plan, out_defs, tm):
    t, d = x.shape
    n = w.shape[1]
    in_specs = [pl.BlockSpec((tm, d), lambda i: (i, 0)),
                pl.BlockSpec((1, d), lambda i: (0, 0)),
                pl.BlockSpec((d, n), lambda i: (0, 0))]
    in_specs += [pl.BlockSpec(a.shape, lambda i: (0, 0)) for a in aux]
    out_shape = [jax.ShapeDtypeStruct((t, wd), dt) for wd, dt in out_defs]
    out_specs = [pl.BlockSpec((tm, wd), lambda i: (i, 0)) for wd, _ in out_defs]
    return pl.pallas_call(
        functools.partial(_proj_kernel, plan=tuple(plan), n_aux=len(aux)),
        grid=(t // tm,), in_specs=in_specs, out_specs=out_specs, out_shape=out_shape,
        compiler_params=_cparams(("parallel",)), name="proj",
    )(x, g.reshape(1, d), w, *aux)


def _oproj_kernel(*refs, n_in):
    x_ref = refs[0]
    a_refs = refs[1:1 + n_in]
    w_refs = refs[1 + n_in:1 + 2 * n_in]
    o_ref = refs[-1]
    acc = x_ref[...]
    for a_ref, w_ref in zip(a_refs, w_refs):
        acc = acc + _dot(a_ref[...], w_ref[...])
    o_ref[...] = acc


def _oproj(x, acts, ws, tm):
    t, d = x.shape
    in_specs = [pl.BlockSpec((tm, d), lambda i: (i, 0))]
    in_specs += [pl.BlockSpec((tm, a.shape[1]), lambda i: (i, 0)) for a in acts]
    in_specs += [pl.BlockSpec(w.shape, lambda i: (0, 0)) for w in ws]
    return pl.pallas_call(
        functools.partial(_oproj_kernel, n_in=len(acts)),
        grid=(t // tm,), in_specs=in_specs,
        out_specs=pl.BlockSpec((tm, d), lambda i: (i, 0)),
        out_shape=jax.ShapeDtypeStruct((t, d), F32),
        compiler_params=_cparams(("parallel",)), name="oproj",
    )(x, *acts, *ws)


def _dn_prep_kernel(qkv_ref, halo_ref, ab_ref, cw_ref, alog_ref, dtb_ref,
                    u_ref, w_ref, qg_ref, qk_ref, kdt_ref, egl_ref, buf_ref, *, lp, seq, dk):
    nh = DN_HEADS
    wq = nh * dk
    first = (pl.program_id(0) * lp) % seq == 0
    buf_ref[0:SUBLANES, :] = jnp.where(first, 0.0, halo_ref[...])
    buf_ref[SUBLANES:SUBLANES + lp, :] = qkv_ref[...]

    ab = ab_ref[...]
    g = -jnp.exp(alog_ref[...]) * _softplus(ab + dtb_ref[...])
    beta = _sigmoid(ab)
    tri = _chunk_tri(lp)
    g1, g2, g3 = _split3(g)
    gc = _dot(tri, g1) + _dot(tri, g2) + _dot(tri, g3)
    gct = gc.T
    nchunk = lp // CHUNK
    gl_rows = jnp.concatenate(
        [jnp.broadcast_to(gc[(c + 1) * CHUNK - 1:(c + 1) * CHUNK, :], (CHUNK, LANES))
         for c in range(nchunk)], axis=0)
    for c in range(nchunk):
        egl_ref[c] = jnp.exp(gc[(c + 1) * CHUNK - 1:(c + 1) * CHUNK, :])

    cw = cw_ref[...]
    kw = cw.shape[0]

    def convsilu(c0):
        y = None
        for i in range(kw):
            r0 = SUBLANES - (kw - 1) + i
            term = cw[i:i + 1, c0:c0 + dk] * buf_ref[r0:r0 + lp, c0:c0 + dk]
            y = term if y is None else y + term
        return _silu(y)

    ii = lax.broadcasted_iota(jnp.int32, (CHUNK, CHUNK), 0)
    jj = lax.broadcasted_iota(jnp.int32, (CHUNK, CHUNK), 1)
    causal = jj <= ii
    strict = jj < ii
    eye = jnp.where(ii == jj, 1.0, 0.0)

    for h in range(nh):
        cs = slice(h * dk, (h + 1) * dk)
        qn = _l2n(convsilu(h * dk)) * (dk ** -0.5)
        kn = _l2n(convsilu(wq + h * dk))
        v = convsilu(2 * wq + h * dk)
        gcol = gc[:, h:h + 1]
        bcol = beta[:, nh + h:nh + h + 1]
        egc = jnp.exp(gcol)
        kb = kn * bcol
        vb = v * bcol
        kbg = (kb * egc).astype(BF16)
        qg_ref[:, cs] = (qn * egc).astype(BF16)
        kdec = kn * jnp.exp(gl_rows[:, h:h + 1] - gcol)
        kdt = kdec.T.astype(BF16)
        knb = kn.astype(BF16)
        kbb = kb.astype(BF16)
        qnb = qn.astype(BF16)
        for c in range(nchunk):
            rs = slice(c * CHUNK, (c + 1) * CHUNK)
            diff = gcol[rs] - gct[h:h + 1, rs]
            decay = jnp.where(causal, jnp.exp(jnp.where(causal, diff, 0.0)), 0.0)
            a_mat = jnp.where(strict, _dot_nt(kbb[rs], knb[rs]) * decay, 0.0)
            x_inv = eye - a_mat
            p = a_mat
            for _ in range(5):
                pb = p.astype(BF16)
                p = _dot(pb, pb)
                x_inv = x_inv + _dot(x_inv.astype(BF16), p.astype(BF16))
            x1, x2 = _split2(x_inv)
            v1, v2 = _split2(vb[rs])
            u_ref[rs, cs] = _dot(x1, v1) + _dot(x1, v2) + _dot(x2, v1)
            w_ref[rs, cs] = _dot(x1, kbg[rs]).astype(BF16)
            qk = jnp.where(causal, _dot_nt(qnb[rs], knb[rs]) * decay, 0.0)
            qk_ref[rs, h * CHUNK:(h + 1) * CHUNK] = qk.astype(BF16)
            kdt_ref[c, :, h * CHUNK:(h + 1) * CHUNK] = kdt[:, rs]


def _dn_prep(qkv, ab, conv_w, alog_row, dtb_row, seq, lp):
    t, wqkv = qkv.shape
    dk = wqkv // (3 * DN_HEADS)
    wq = DN_HEADS * dk
    hb = lp // SUBLANES
    nchunk = lp // CHUNK
    out_shape = [jax.ShapeDtypeStruct((t, wq), F32),
                 jax.ShapeDtypeStruct((t, wq), BF16),
                 jax.ShapeDtypeStruct((t, wq), BF16),
                 jax.ShapeDtypeStruct((t, DN_HEADS * CHUNK), BF16),
                 jax.ShapeDtypeStruct((t // CHUNK, dk, DN_HEADS * CHUNK), BF16),
                 jax.ShapeDtypeStruct((t // CHUNK, 1, LANES), F32)]
    out_specs = [pl.BlockSpec((lp, wq), lambda i: (i, 0)),
                 pl.BlockSpec((lp, wq), lambda i: (i, 0)),
                 pl.BlockSpec((lp, wq), lambda i: (i, 0)),
                 pl.BlockSpec((lp, DN_HEADS * CHUNK), lambda i: (i, 0)),
                 pl.BlockSpec((nchunk, dk, DN_HEADS * CHUNK), lambda i: (i, 0, 0)),
                 pl.BlockSpec((nchunk, 1, LANES), lambda i: (i, 0, 0))]
    in_specs = [pl.BlockSpec((lp, wqkv), lambda i: (i, 0)),
                pl.BlockSpec((SUBLANES, wqkv), lambda i: (jnp.maximum(i * hb - 1, 0), 0)),
                pl.BlockSpec((lp, LANES), lambda i: (i, 0)),
                pl.BlockSpec(conv_w.shape, lambda i: (0, 0)),
                pl.BlockSpec((1, LANES), lambda i: (0, 0)),
                pl.BlockSpec((1, LANES), lambda i: (0, 0))]
    return pl.pallas_call(
        functools.partial(_dn_prep_kernel, lp=lp, seq=seq, dk=dk),
        grid=(t // lp,), in_specs=in_specs, out_specs=out_specs, out_shape=out_shape,
        scratch_shapes=[pltpu.VMEM((lp + SUBLANES, wqkv), F32)],
        compiler_params=_cparams(("parallel",)), name="dn_prep",
    )(qkv, qkv, ab, conv_w, alog_row, dtb_row)


def _dn_scan_kernel(u_ref, w_ref, qg_ref, qk_ref, kdt_ref, egl_ref, z_ref, gain_ref,
                    o_ref, s_ref, *, ls, dk):
    @pl.when(pl.program_id(1) == 0)
    def _():
        s_ref[...] = jnp.zeros_like(s_ref)

    gain = gain_ref[...]

    def body(c, carry):
        r0 = pl.multiple_of(c * CHUNK, CHUNK)
        rows = pl.ds(r0, CHUNK)
        e_full = jnp.broadcast_to(egl_ref[c], (dk, LANES))
        for h in range(DN_HEADS):
            cs = slice(h * dk, (h + 1) * dk)
            hs = slice(h * CHUNK, (h + 1) * CHUNK)
            s = s_ref[h]
            sb = s.astype(BF16)
            delta = u_ref[rows, cs] - _dot(w_ref[rows, cs], sb)
            db = delta.astype(BF16)
            o = _dot(qg_ref[rows, cs], sb) + _dot(qk_ref[rows, hs], db)
            s_ref[h] = e_full[:, h:h + 1] * s + _dot(kdt_ref[c, :, hs], db)
            o_ref[rows, cs] = (_rms(o, gain) * _silu(z_ref[rows, cs])).astype(BF16)
        return carry

    lax.fori_loop(0, ls // CHUNK, body, 0)


def _dn_scan(u, w, qg, qk, kdt, egl, z, gain, batch, seq, ls):
    t, wq = u.shape
    dk = wq // DN_HEADS
    ns = seq // ls
    nchunk = ls // CHUNK
    tok = lambda wd: pl.BlockSpec((ls, wd), lambda b, s: (b * ns + s, 0))
    in_specs = [tok(wq), tok(wq), tok(wq), tok(DN_HEADS * CHUNK),
                pl.BlockSpec((nchunk, dk, DN_HEADS * CHUNK), lambda b, s: (b * ns + s, 0, 0)),
                pl.BlockSpec((nchunk, 1, LANES), lambda b, s: (b * ns + s, 0, 0)),
                tok(wq),
                pl.BlockSpec((1, dk), lambda b, s: (0, 0))]
    return pl.pallas_call(
        functools.partial(_dn_scan_kernel, ls=ls, dk=dk),
        grid=(batch, ns), in_specs=in_specs, out_specs=tok(wq),
        out_shape=jax.ShapeDtypeStruct((t, wq), BF16),
        scratch_shapes=[pltpu.VMEM((DN_HEADS, dk, dk), F32)],
        compiler_params=_cparams(("parallel", "arbitrary")), name="dn_scan",
    )(u, w, qg, qk, kdt, egl, z, gain)


def _sb_kernel(q_ref, k_ref, v_ref, o_ref, acc_ref, r_ref, *, tq, scale):
    qi = pl.program_id(2)
    q = q_ref[...]
    rr = lax.broadcasted_iota(jnp.int32, (tq, tq), 0)
    cc = lax.broadcasted_iota(jnp.int32, (tq, tq), 1)
    suffix = jnp.where(rr >= cc, 1.0, 0.0).astype(BF16)
    strict = cc < rr

    acc_ref[...] = jnp.zeros_like(acc_ref)
    r_ref[...] = jnp.zeros_like(r_ref)

    def tile(kt, masked):
        k0 = pl.multiple_of(kt * tq, tq)
        kk = k_ref[pl.ds(k0, tq), :]
        vv = v_ref[pl.ds(k0, tq), :]
        z = _dot_nt(q, kk) * scale
        l = -_softplus(z)
        if masked:
            l = jnp.where(strict, l, 0.0)
        lhi, llo = _split2(l)
        csum = _dot(lhi, suffix) + _dot(llo, suffix)
        att = jnp.exp(z + csum + r_ref[...])
        if masked:
            att = jnp.where(strict, att, 0.0)
        acc_ref[...] += _dot(att.astype(BF16), vv)
        r_ref[...] += csum[:, 0:1]

    tile(qi, True)

    def body(j, carry):
        tile(qi - 1 - j, False)
        return carry

    lax.fori_loop(0, qi, body, 0)
    o_ref[...] = acc_ref[...].astype(BF16)


def _sb_attention(q, k, v, batch, seq, tq):
    t, wq = q.shape
    dh = wq // SB_HEADS
    nq = seq // tq
    return pl.pallas_call(
        functools.partial(_sb_kernel, tq=tq, scale=dh ** -0.5),
        grid=(batch, SB_HEADS, nq),
        in_specs=[pl.BlockSpec((tq, dh), lambda b, h, i: (b * nq + i, h)),
                  pl.BlockSpec((seq, dh), lambda b, h, i: (b, h)),
                  pl.BlockSpec((seq, dh), lambda b, h, i: (b, h))],
        out_specs=pl.BlockSpec((tq, dh), lambda b, h, i: (b * nq + i, h)),
        out_shape=jax.ShapeDtypeStruct((t, wq), BF16),
        scratch_shapes=[pltpu.VMEM((tq, dh), F32), pltpu.VMEM((tq, 1), F32)],
        compiler_params=_cparams(("parallel", "parallel", "arbitrary")), name="sb_attn",
    )(q, k, v)


def _gla_kernel(q_ref, k_ref, gk_ref, v_ref, r_ref, gain_ref, o_ref, s_ref, *, ls, dk, dv):
    @pl.when(pl.program_id(1) == 0)
    def _():
        s_ref[...] = jnp.zeros_like(s_ref)

    gain = gain_ref[...]
    tri = _chunk_tri(ls)
    g1, g2, g3 = _split3(gk_ref[...])
    bc_all = _dot(tri, g1) + _dot(tri, g2) + _dot(tri, g3)
    ii = lax.broadcasted_iota(jnp.int32, (CHUNK, CHUNK), 0)
    jj = lax.broadcasted_iota(jnp.int32, (CHUNK, CHUNK), 1)
    causal = jj <= ii
    mid = CHUNK // 2
    for c in range(ls // CHUNK):
        rs = slice(c * CHUNK, (c + 1) * CHUNK)
        for h in range(GLA_HEADS):
            ks = slice(h * dk, (h + 1) * dk)
            vs = slice(h * dv, (h + 1) * dv)
            bc = bc_all[rs, ks]
            q = q_ref[rs, ks] * (dk ** -0.5)
            k = k_ref[rs, ks]
            v = v_ref[rs, vs]
            bmid = bc[mid:mid + 1, :]
            blast = bc[CHUNK - 1:CHUNK, :]
            att = _dot_nt((q * jnp.exp(bc - bmid)).astype(BF16), (k * jnp.exp(bmid - bc)).astype(BF16))
            att = jnp.where(causal, att, 0.0)
            s = s_ref[h]
            o = _dot(att.astype(BF16), v) + _dot((q * jnp.exp(bc)).astype(BF16), s.astype(BF16))
            kdec_t = (k * jnp.exp(blast - bc)).T
            eb_col = jnp.exp(bc[CHUNK - SUBLANES:CHUNK, :]).T[:, SUBLANES - 1:SUBLANES]
            s_ref[h] = eb_col * s + _dot(kdec_t.astype(BF16), v)
            rr = r_ref[rs, vs]
            o_ref[rs, vs] = (_rms(o, gain) * _silu(rr)).astype(BF16)


def _gla(q, k, gk, v, r, gain, batch, seq, ls):
    t, wk = q.shape
    wv = v.shape[1]
    dk = wk // GLA_HEADS
    dv = wv // GLA_HEADS
    ns = seq // ls
    tok = lambda wd: pl.BlockSpec((ls, wd), lambda b, s: (b * ns + s, 0))
    return pl.pallas_call(
        functools.partial(_gla_kernel, ls=ls, dk=dk, dv=dv),
        grid=(batch, ns),
        in_specs=[tok(wk), tok(wk), tok(wk), tok(wv), tok(wv),
                  pl.BlockSpec((1, dv), lambda b, s: (0, 0))],
        out_specs=tok(wv),
        out_shape=jax.ShapeDtypeStruct((t, wv), BF16),
        scratch_shapes=[pltpu.VMEM((GLA_HEADS, dk, dv), F32)],
        compiler_params=_cparams(("parallel", "arbitrary")), name="gla",
    )(q, k, gk, v, r, gain)


def _xa_kernel(x_ref, g_ref, wq_ref, gq_ref, k_ref, v_ref, wo_ref, o_ref, *, dh):
    x = x_ref[...]
    hn = _rms(x, g_ref[...]).astype(BF16)
    gq = gq_ref[...]
    acc = x
    for h in range(XA_HEADS):
        hs = slice(h * dh, (h + 1) * dh)
        qh = _rms(_dot(hn, wq_ref[:, hs]), gq).astype(BF16)
        sc = _dot_nt(qh, k_ref[:, hs]) * (dh ** -0.5)
        e = jnp.exp(sc - jnp.max(sc, axis=-1, keepdims=True))
        pr = e / jnp.sum(e, axis=-1, keepdims=True)
        oh = _dot(pr.astype(BF16), v_ref[:, hs])
        acc = acc + _dot(oh.astype(BF16), wo_ref[hs, :])
    o_ref[...] = acc


def _xattn(x, g, wq, gq, kmem, vmem, wo, seq, tm):
    t, d = x.shape
    dh = d // XA_HEADS
    nm = kmem.shape[0] // (t // seq)
    per = seq // tm
    return pl.pallas_call(
        functools.partial(_xa_kernel, dh=dh),
        grid=(t // tm,),
        in_specs=[pl.BlockSpec((tm, d), lambda i: (i, 0)),
                  pl.BlockSpec((1, d), lambda i: (0, 0)),
                  pl.BlockSpec((d, d), lambda i: (0, 0)),
                  pl.BlockSpec((1, dh), lambda i: (0, 0)),
                  pl.BlockSpec((nm, d), lambda i: (i // per, 0)),
                  pl.BlockSpec((nm, d), lambda i: (i // per, 0)),
                  pl.BlockSpec((d, d), lambda i: (0, 0))],
        out_specs=pl.BlockSpec((tm, d), lambda i: (i, 0)),
        out_shape=jax.ShapeDtypeStruct((t, d), F32),
        compiler_params=_cparams(("parallel",)), name="xattn",
    )(x, g.reshape(1, d), wq, gq.reshape(1, dh), kmem, vmem, wo)


def _ffn_kernel(x_ref, g_ref, wg_ref, wv_ref, cwg_ref, cwv_ref, cbg_ref, cbv_ref, wd_ref,
                o_ref, hn_ref, acc_ref, ubuf_ref, carry_ref, *, tm, seq, nj):
    i = pl.program_id(0)
    j = pl.program_id(1)

    @pl.when(j == 0)
    def _():
        hn_ref[...] = _rms(x_ref[...], g_ref[...]).astype(BF16)
        acc_ref[...] = jnp.zeros_like(acc_ref)

    first = (i * tm) % seq == 0
    hn = hn_ref[...]

    def branch(w_ref, cw_ref, cb_ref, slot):
        u = _dot(hn, w_ref[...])
        ubuf_ref[slot, 0:SUBLANES, :] = jnp.where(first, 0.0, carry_ref[j, slot])
        ubuf_ref[slot, SUBLANES:SUBLANES + tm, :] = u
        carry_ref[j, slot] = u[tm - SUBLANES:tm, :]
        cw = cw_ref[...]
        kw = cw.shape[0]
        y = cb_ref[...] + cw[kw - 1:kw, :] * u
        for s in range(1, kw):
            y = y + cw[kw - 1 - s:kw - s, :] * ubuf_ref[slot, SUBLANES - s:SUBLANES - s + tm, :]
        return y

    gate = branch(wg_ref, cwg_ref, cbg_ref, 0)
    val = branch(wv_ref, cwv_ref, cbv_ref, 1)
    act = (_silu(gate) * val).astype(BF16)
    acc_ref[...] += _dot(act, wd_ref[...])

    @pl.when(j == nj - 1)
    def _():
        o_ref[...] = x_ref[...] + acc_ref[...]


def _ffn(x, g, w_up, conv_w, conv_b, w_down, seq, tm, tff):
    t, d = x.shape
    dff = w_down.shape[0]
    nj = dff // tff
    kw = conv_w.shape[0]
    return pl.pallas_call(
        functools.partial(_ffn_kernel, tm=tm, seq=seq, nj=nj),
        grid=(t // tm, nj),
        in_specs=[pl.BlockSpec((tm, d), lambda i, j: (i, 0)),
                  pl.BlockSpec((1, d), lambda i, j: (0, 0)),
                  pl.BlockSpec((d, tff), lambda i, j: (0, j)),
                  pl.BlockSpec((d, tff), lambda i, j: (0, nj + j)),
                  pl.BlockSpec((kw, tff), lambda i, j: (0, j)),
                  pl.BlockSpec((kw, tff), lambda i, j: (0, nj + j)),
                  pl.BlockSpec((1, tff), lambda i, j: (0, j)),
                  pl.BlockSpec((1, tff), lambda i, j: (0, nj + j)),
                  pl.BlockSpec((tff, d), lambda i, j: (j, 0))],
        out_specs=pl.BlockSpec((tm, d), lambda i, j: (i, 0)),
        out_shape=jax.ShapeDtypeStruct((t, d), F32),
        scratch_shapes=[pltpu.VMEM((tm, d), BF16),
                        pltpu.VMEM((tm, d), F32),
                        pltpu.VMEM((2, tm + SUBLANES, tff), F32),
                        pltpu.VMEM((nj, 2, SUBLANES, tff), F32)],
        compiler_params=_cparams(("arbitrary", "arbitrary")), name="convffn",
    )(x, g.reshape(1, d), w_up, w_up, conv_w, conv_w, conv_b.reshape(1, -1), conv_b.reshape(1, -1), w_down)


def _pad_cols(w, n):
    return jnp.pad(w, ((0, 0), (0, n - w.shape[1])))


def _mixer_ab(x, g, w_in, conv_w, a_log, dt_bias, dn_g, sb_gq, sb_gk, w_out, batch, seq):
    dk = dn_g.shape[0]
    wq = DN_HEADS * dk
    dh = sb_gq.shape[0]
    wb = SB_HEADS * dh
    c_ab = 4 * wq
    w = jnp.concatenate([w_in[:, :c_ab], _pad_cols(w_in[:, c_ab:c_ab + 2 * DN_HEADS], LANES),
                         w_in[:, c_ab + 2 * DN_HEADS:]], axis=1).astype(BF16)
    o_qb = c_ab + LANES
    plan = [(0, 3 * wq, "raw", 0), (3 * wq, wq, "raw", 0), (c_ab, LANES, "raw", 0),
            (o_qb, wb, "headnorm", dh), (o_qb + wb, wb, "headnorm", dh), (o_qb + 2 * wb, wb, "raw", 0)]
    outs = [(3 * wq, F32), (wq, F32), (LANES, F32), (wb, BF16), (wb, BF16), (wb, BF16)]
    qkv_a, z_a, ab, q_b, k_b, v_b = _proj(
        x, g, w, [sb_gq.reshape(1, dh), sb_gk.reshape(1, dh)], plan, outs, tm=512)

    alog_row = _pad_cols(a_log.reshape(1, -1), LANES)
    dtb_row = _pad_cols(dt_bias.reshape(1, -1), LANES)
    u, wmat, qg, qk, kdt, egl = _dn_prep(qkv_a, ab, conv_w, alog_row, dtb_row, seq, lp=256)
    o_a = _dn_scan(u, wmat, qg, qk, kdt, egl, z_a, dn_g.reshape(1, dk), batch, seq, ls=512)
    o_b = _sb_attention(q_b, k_b, v_b, batch, seq, tq=256)
    wo = w_out.astype(BF16)
    return _oproj(x, [o_a, o_b], [wo[:wq], wo[wq:]], tm=512)


def _mixer_gla(x, g, w_in, w_gk, b_gk, norm_g, w_out, batch, seq):
    rank, wk = w_gk.shape
    dv = norm_g.shape[0]
    wv = GLA_HEADS * dv
    w = _pad_cols(w_in, 2 * wk + 2 * wv + LANES).astype(BF16)
    wgk = jnp.pad(w_gk, ((0, LANES - rank), (0, 0))).astype(BF16)
    plan = [(0, wk, "raw", 0), (wk, wk, "raw", 0), (2 * wk, wv, "raw", 0),
            (2 * wk + wv, wv, "raw", 0), (2 * wk + 2 * wv, LANES, "gk", 0)]
    outs = [(wk, F32), (wk, F32), (wv, BF16), (wv, F32), (wk, F32)]
    q, k, v, r, gk = _proj(x, g, w, [wgk, b_gk.reshape(1, wk)], plan, outs, tm=512)
    o = _gla(q, k, gk, v, r, norm_g.reshape(1, dv), batch, seq, ls=256)
    return _oproj(x, [o], [w_out.astype(BF16)], tm=512)


def _mem_kv(mem, g, w_kv, g_k):
    d = mem.shape[1]
    dh = g_k.shape[0]
    plan = [(0, d, "headnorm", dh), (d, d, "raw", 0)]
    outs = [(d, BF16), (d, BF16)]
    return _proj(mem, g, w_kv.astype(BF16), [g_k.reshape(1, dh)], plan, outs, tm=256)


def kernel(x, mem, norm_mix, norm_xa, norm_mem, norm_ffn, xa_w_q, xa_w_kv, xa_w_o, xa_g_q, xa_g_k, ffn_w_up, ffn_conv_w, ffn_conv_b, ffn_w_down, ab_w_in, ab_conv_w, dn_a_log, dn_dt_bias, dn_norm_g, sb_g_q, sb_g_k, ab_w_out, gla_w_in, gla_w_gk, gla_b_gk, gla_norm_g, gla_w_out):
    batch, seq, d = x.shape
    depth = norm_mix.shape[0]
    xf = x.reshape(batch * seq, d)
    memf = mem.reshape(batch * mem.shape[1], d)
    for layer in range(depth):
        i = layer // 2
        if layer % 2 == 0:
            xf = _mixer_ab(xf, norm_mix[layer], ab_w_in[i], ab_conv_w[i], dn_a_log[i], dn_dt_bias[i],
                           dn_norm_g[i], sb_g_q[i], sb_g_k[i], ab_w_out[i], batch, seq)
        else:
            xf = _mixer_gla(xf, norm_mix[layer], gla_w_in[i], gla_w_gk[i], gla_b_gk[i],
                            gla_norm_g[i], gla_w_out[i], batch, seq)
        kmem, vmem = _mem_kv(memf, norm_mem[layer], xa_w_kv[layer], xa_g_k[layer])
        xf = _xattn(xf, norm_xa[layer], xa_w_q[layer].astype(BF16), xa_g_q[layer], kmem, vmem,
                    xa_w_o[layer].astype(BF16), seq, tm=512)
        xf = _ffn(xf, norm_ffn[layer], ffn_w_up[layer].astype(BF16), ffn_conv_w[layer],
                  ffn_conv_b[layer], ffn_w_down[layer].astype(BF16), seq, tm=512, tff=256)
    return xf.reshape(batch, seq, d)
```

```python
import functools

import jax
import jax.numpy as jnp
from jax import lax
from jax.experimental import pallas as pl
from jax.experimental.pallas import tpu as pltpu

F32 = jnp.float32
BF16 = jnp.bfloat16
EPS = 1e-6

DN_HEADS = 4
SB_HEADS = 4
GLA_HEADS = 4
XA_HEADS = 4
GLA_TAU = 16.0
CHUNK = 64
LANES = 128
SUBLANES = 8
VMEM_LIMIT = 56 * 1024 * 1024
SB_UNDERFLOW = 110.0


def _cparams(sem):
    return pltpu.CompilerParams(dimension_semantics=sem, vmem_limit_bytes=VMEM_LIMIT)


def _rms(x, g):
    ms = jnp.mean(x * x, axis=-1, keepdims=True)
    return x * lax.rsqrt(ms + EPS) * g


def _l2n(x):
    return x * lax.rsqrt(jnp.sum(x * x, axis=-1, keepdims=True) + EPS)


def _dot(a, b):
    return jnp.dot(a, b, preferred_element_type=F32)


def _dot_nt(a, b):
    return lax.dot_general(a, b, (((1,), (1,)), ((), ())), preferred_element_type=F32)


def _split2(x):
    hi = x.astype(BF16)
    lo = (x - hi.astype(F32)).astype(BF16)
    return hi, lo


def _split3(x):
    hi = x.astype(BF16)
    r = x - hi.astype(F32)
    mid = r.astype(BF16)
    lo = (r - mid.astype(F32)).astype(BF16)
    return hi, mid, lo


def _sigmoid(x):
    return 1.0 / (1.0 + jnp.exp(-x))


def _silu(x):
    return x * _sigmoid(x)


def _softplus(x):
    return jnp.maximum(x, 0.0) + jnp.log1p(jnp.exp(-jnp.abs(x)))


def _chunk_tri(n):
    r = lax.broadcasted_iota(jnp.int32, (n, n), 0)
    c = lax.broadcasted_iota(jnp.int32, (n, n), 1)
    same = (r // CHUNK) == (c // CHUNK)
    return jnp.where(same & (c <= r), 1.0, 0.0).astype(BF16)


def _proj_kernel(*refs, plan, n_aux):
    x_ref, g_ref, w_ref = refs[:3]
    aux = refs[3:3 + n_aux]
    outs = refs[3 + n_aux:]
    hn = _rms(x_ref[...], g_ref[...]).astype(BF16)
    ai = 0
    for (c0, width, mode, hd), o_ref in zip(plan, outs):
        if mode == "headnorm":
            gain = aux[ai][...]
            ai += 1
        elif mode == "gk":
            wgk_ref, bgk_ref = aux[ai], aux[ai + 1]
            ai += 2
        step = min(width, 512)
        for s0 in range(0, width, step):
            p = _dot(hn, w_ref[:, c0 + s0:c0 + s0 + step])
            if mode == "raw":
                o_ref[:, s0:s0 + step] = p.astype(o_ref.dtype)
            elif mode == "headnorm":
                for h0 in range(0, step, hd):
                    o_ref[:, s0 + h0:s0 + h0 + hd] = _rms(p[:, h0:h0 + hd], gain).astype(o_ref.dtype)
            else:
                y = _dot(p.astype(BF16), wgk_ref[...]) + bgk_ref[...]
                o_ref[...] = (jnp.minimum(y, 0.0) - jnp.log1p(jnp.exp(-jnp.abs(y)))) / GLA_TAU


def _proj(x, g, w, aux, plan, out_defs, tm):
    t, d = x.shape
    n = w.shape[1]
    in_specs = [pl.BlockSpec((tm, d), lambda i: (i, 0)),
                pl.BlockSpec((1, d), lambda i: (0, 0)),
                pl.BlockSpec((d, n), lambda i: (0, 0))]
    in_specs += [pl.BlockSpec(a.shape, lambda i: (0, 0)) for a in aux]
    out_shape = [jax.ShapeDtypeStruct((t, wd), dt) for wd, dt in out_defs]
    out_specs = [pl.BlockSpec((tm, wd), lambda i: (i, 0)) for wd, _ in out_defs]
    return pl.pallas_call(
        functools.partial(_proj_kernel, plan=tuple(plan), n_aux=len(aux)),
        grid=(t // tm,), in_specs=in_specs, out_specs=out_specs, out_shape=out_shape,
        compiler_params=_cparams(("parallel",)), name="proj",
    )(x, g.reshape(1, d), w, *aux)


def _oproj_kernel(*refs, n_in):
    x_ref = refs[0]
    a_refs = refs[1:1 + n_in]
    w_refs = refs[1 + n_in:1 + 2 * n_in]
    o_ref = refs[-1]
    acc = x_ref[...]
    for a_ref, w_ref in zip(a_refs, w_refs):
        acc = acc + _dot(a_ref[...], w_ref[...])
    o_ref[...] = acc


def _oproj(x, acts, ws, tm):
    t, d = x.shape
    in_specs = [pl.BlockSpec((tm, d), lambda i: (i, 0))]
    in_specs += [pl.BlockSpec((tm, a.shape[1]), lambda i: (i, 0)) for a in acts]
    in_specs += [pl.BlockSpec(w.shape, lambda i: (0, 0)) for w in ws]
    return pl.pallas_call(
        functools.partial(_oproj_kernel, n_in=len(acts)),
        grid=(t // tm,), in_specs=in_specs,
        out_specs=pl.BlockSpec((tm, d), lambda i: (i, 0)),
        out_shape=jax.ShapeDtypeStruct((t, d), F32),
        compiler_params=_cparams(("parallel",)), name="oproj",
    )(x, *acts, *ws)


def _dn_prep_kernel(qkv_ref, halo_ref, ab_ref, cw_ref, alog_ref, dtb_ref,
                    u_ref, w_ref, qg_ref, qk_ref, kdt_ref, egl_ref, buf_ref, *, lp, seq, dk):
    nh = DN_HEADS
    wq = nh * dk
    first = (pl.program_id(0) * lp) % seq == 0
    buf_ref[0:SUBLANES, :] = jnp.where(first, 0.0, halo_ref[...])
    buf_ref[SUBLANES:SUBLANES + lp, :] = qkv_ref[...]

    ab = ab_ref[...]
    g = -jnp.exp(alog_ref[...]) * _softplus(ab + dtb_ref[...])
    beta = _sigmoid(ab)
    tri = _chunk_tri(lp)
    g1, g2, g3 = _split3(g)
    gc = _dot(tri, g1) + _dot(tri, g2) + _dot(tri, g3)
    gct = gc.T
    nchunk = lp // CHUNK
    gl_rows = jnp.concatenate(
        [jnp.broadcast_to(gc[(c + 1) * CHUNK - 1:(c + 1) * CHUNK, :], (CHUNK, LANES))
         for c in range(nchunk)], axis=0)
    for c in range(nchunk):
        egl_ref[c] = jnp.exp(gc[(c + 1) * CHUNK - 1:(c + 1) * CHUNK, :])

    cw = cw_ref[...]
    kw = cw.shape[0]

    def convsilu(c0):
        y = None
        for i in range(kw):
            r0 = SUBLANES - (kw - 1) + i
            term = cw[i:i + 1, c0:c0 + dk] * buf_ref[r0:r0 + lp, c0:c0 + dk]
            y = term if y is None else y + term
        return _silu(y)

    ii = lax.broadcasted_iota(jnp.int32, (lp, lp), 0)
    jj = lax.broadcasted_iota(jnp.int32, (lp, lp), 1)
    same = (ii // CHUNK) == (jj // CHUNK)
    causal = same & (jj <= ii)
    strict = same & (jj < ii)
    eye = jnp.where(ii == jj, 1.0, 0.0)

    a_mats, vbs, kbgs = [], [], []
    for h in range(nh):
        cs = slice(h * dk, (h + 1) * dk)
        qn = _l2n(convsilu(h * dk)) * (dk ** -0.5)
        kn = _l2n(convsilu(wq + h * dk))
        v = convsilu(2 * wq + h * dk)
        gcol = gc[:, h:h + 1]
        bcol = beta[:, nh + h:nh + h + 1]
        egc = jnp.exp(gcol)
        kb = kn * bcol
        vbs.append(v * bcol)
        kbgs.append((kb * egc).astype(BF16))
        qg_ref[:, cs] = (qn * egc).astype(BF16)
        kdec = kn * jnp.exp(gl_rows[:, h:h + 1] - gcol)
        kdt = kdec.T.astype(BF16)
        knb = kn.astype(BF16)
        diff = gcol - gct[h:h + 1, :]
        decay = jnp.where(causal, jnp.exp(jnp.where(causal, diff, 0.0)), 0.0)
        a_mats.append(jnp.where(strict, _dot_nt(kb.astype(BF16), knb) * decay, 0.0))
        qk = (_dot_nt(qn.astype(BF16), knb) * decay).astype(BF16)
        for c in range(nchunk):
            rs = slice(c * CHUNK, (c + 1) * CHUNK)
            qk_ref[rs, h * CHUNK:(h + 1) * CHUNK] = qk[rs, rs]
            kdt_ref[c, :, h * CHUNK:(h + 1) * CHUNK] = kdt[:, rs]

    x_invs = [eye - a for a in a_mats]
    ps = a_mats
    for _ in range(5):
        pbs = [p.astype(BF16) for p in ps]
        ps = [_dot(pb, pb) for pb in pbs]
        x_invs = [x + _dot(x.astype(BF16), p.astype(BF16)) for x, p in zip(x_invs, ps)]
    for h in range(nh):
        cs = slice(h * dk, (h + 1) * dk)
        x1, x2 = _split2(x_invs[h])
        v1, v2 = _split2(vbs[h])
        u_ref[:, cs] = _dot(x1, v1) + _dot(x1, v2) + _dot(x2, v1)
        w_ref[:, cs] = _dot(x1, kbgs[h]).astype(BF16)


def _dn_prep(qkv, ab, conv_w, alog_row, dtb_row, seq, lp):
    t, wqkv = qkv.shape
    dk = wqkv // (3 * DN_HEADS)
    wq = DN_HEADS * dk
    hb = lp // SUBLANES
    nchunk = lp // CHUNK
    out_shape = [jax.ShapeDtypeStruct((t, wq), F32),
                 jax.ShapeDtypeStruct((t, wq), BF16),
                 jax.ShapeDtypeStruct((t, wq), BF16),
                 jax.ShapeDtypeStruct((t, DN_HEADS * CHUNK), BF16),
                 jax.ShapeDtypeStruct((t // CHUNK, dk, DN_HEADS * CHUNK), BF16),
                 jax.ShapeDtypeStruct((t // CHUNK, 1, LANES), F32)]
    out_specs = [pl.BlockSpec((lp, wq), lambda i: (i, 0)),
                 pl.BlockSpec((lp, wq), lambda i: (i, 0)),
                 pl.BlockSpec((lp, wq), lambda i: (i, 0)),
                 pl.BlockSpec((lp, DN_HEADS * CHUNK), lambda i: (i, 0)),
                 pl.BlockSpec((nchunk, dk, DN_HEADS * CHUNK), lambda i: (i, 0, 0)),
                 pl.BlockSpec((nchunk, 1, LANES), lambda i: (i, 0, 0))]
    in_specs = [pl.BlockSpec((lp, wqkv), lambda i: (i, 0)),
                pl.BlockSpec((SUBLANES, wqkv), lambda i: (jnp.maximum(i * hb - 1, 0), 0)),
                pl.BlockSpec((lp, LANES), lambda i: (i, 0)),
                pl.BlockSpec(conv_w.shape, lambda i: (0, 0)),
                pl.BlockSpec((1, LANES), lambda i: (0, 0)),
                pl.BlockSpec((1, LANES), lambda i: (0, 0))]
    return pl.pallas_call(
        functools.partial(_dn_prep_kernel, lp=lp, seq=seq, dk=dk),
        grid=(t // lp,), in_specs=in_specs, out_specs=out_specs, out_shape=out_shape,
        scratch_shapes=[pltpu.VMEM((lp + SUBLANES, wqkv), F32)],
        compiler_params=_cparams(("parallel",)), name="dn_prep",
    )(qkv, qkv, ab, conv_w, alog_row, dtb_row)


def _dn_scan_kernel(u_ref, w_ref, qg_ref, qk_ref, kdt_ref, egl_ref, z_ref, gain_ref,
                    o_ref, s_ref, *, ls, dk):
    @pl.when(pl.program_id(1) == 0)
    def _():
        s_ref[...] = jnp.zeros_like(s_ref)

    gain = gain_ref[...]

    def body(c, carry):
        r0 = pl.multiple_of(c * CHUNK, CHUNK)
        rows = pl.ds(r0, CHUNK)
        e_full = jnp.broadcast_to(egl_ref[c], (dk, LANES))
        for h in range(DN_HEADS):
            cs = slice(h * dk, (h + 1) * dk)
            hs = slice(h * CHUNK, (h + 1) * CHUNK)
            s = s_ref[h]
            sb = s.astype(BF16)
            delta = u_ref[rows, cs] - _dot(w_ref[rows, cs], sb)
            db = delta.astype(BF16)
            o = _dot(qg_ref[rows, cs], sb) + _dot(qk_ref[rows, hs], db)
            s_ref[h] = e_full[:, h:h + 1] * s + _dot(kdt_ref[c, :, hs], db)
            o_ref[rows, cs] = (_rms(o, gain) * _silu(z_ref[rows, cs])).astype(BF16)
        return carry

    lax.fori_loop(0, ls // CHUNK, body, 0)


def _dn_scan(u, w, qg, qk, kdt, egl, z, gain, batch, seq, ls):
    t, wq = u.shape
    dk = wq // DN_HEADS
    ns = seq // ls
    nchunk = ls // CHUNK
    tok = lambda wd: pl.BlockSpec((ls, wd), lambda b, s: (b * ns + s, 0))
    in_specs = [tok(wq), tok(wq), tok(wq), tok(DN_HEADS * CHUNK),
                pl.BlockSpec((nchunk, dk, DN_HEADS * CHUNK), lambda b, s: (b * ns + s, 0, 0)),
                pl.BlockSpec((nchunk, 1, LANES), lambda b, s: (b * ns + s, 0, 0)),
                tok(wq),
                pl.BlockSpec((1, dk), lambda b, s: (0, 0))]
    return pl.pallas_call(
        functools.partial(_dn_scan_kernel, ls=ls, dk=dk),
        grid=(batch, ns), in_specs=in_specs, out_specs=tok(wq),
        out_shape=jax.ShapeDtypeStruct((t, wq), BF16),
        scratch_shapes=[pltpu.VMEM((DN_HEADS, dk, dk), F32)],
        compiler_params=_cparams(("parallel", "arbitrary")), name="dn_scan",
    )(u, w, qg, qk, kdt, egl, z, gain)


def _sb_kernel(q_ref, k_ref, v_ref, o_ref, acc_ref, r_ref, kmax_ref, *, tq, scale):
    qi = pl.program_id(2)
    half = tq // 2

    @pl.when(qi == 0)
    def _():
        kf = k_ref[...].astype(F32)
        kn2 = jnp.sum(kf * kf, axis=-1, keepdims=True)
        kmax_ref[...] = jnp.broadcast_to(jnp.sqrt(jnp.max(kn2, axis=0, keepdims=True)), kmax_ref.shape)

    q = q_ref[...]
    qf = q.astype(F32)
    zb = scale * jnp.sqrt(jnp.sum(qf * qf, axis=-1, keepdims=True)) * kmax_ref[0:1, 0:1]
    rr = lax.broadcasted_iota(jnp.int32, (tq, tq), 0)
    cc = lax.broadcasted_iota(jnp.int32, (tq, tq), 1)
    suffix = jnp.where(rr >= cc, 1.0, 0.0).astype(BF16)

    acc_ref[...] = jnp.zeros_like(acc_ref)
    r_ref[...] = jnp.zeros_like(r_ref)

    def tile(kt, masked):
        k0 = pl.multiple_of(kt * tq, tq)
        kk = k_ref[pl.ds(k0, tq), :]
        vv = v_ref[pl.ds(k0, tq), :]
        for hb in range(2):
            rs = slice(hb * half, (hb + 1) * half)
            z = _dot_nt(q[rs], kk) * scale
            l = -_softplus(z)
            if masked:
                tpos = lax.broadcasted_iota(jnp.int32, (half, tq), 0) + hb * half
                strict = lax.broadcasted_iota(jnp.int32, (half, tq), 1) < tpos
                l = jnp.where(strict, l, 0.0)
            lhi, llo = _split2(l)
            csum = _dot(lhi, suffix) + _dot(llo, suffix)
            att = jnp.exp(z + csum + r_ref[rs, :])
            if masked:
                att = jnp.where(strict, att, 0.0)
            acc_ref[rs, :] += _dot(att.astype(BF16), vv)
            r_ref[rs, :] += csum[:, 0:1]

    def live():
        return jnp.max(r_ref[...] + zb)

    tile(qi, True)

    def cond(carry):
        j, m = carry
        return jnp.logical_and(j < qi, m > -SB_UNDERFLOW)

    def body(carry):
        j, _ = carry
        tile(qi - 1 - j, False)
        return j + 1, live()

    lax.while_loop(cond, body, (jnp.int32(0), live()))
    o_ref[...] = acc_ref[...].astype(BF16)


def _sb_attention(q, k, v, batch, seq, tq):
    t, wq = q.shape
    dh = wq // SB_HEADS
    nq = seq // tq
    return pl.pallas_call(
        functools.partial(_sb_kernel, tq=tq, scale=dh ** -0.5),
        grid=(batch, SB_HEADS, nq),
        in_specs=[pl.BlockSpec((tq, dh), lambda b, h, i: (b * nq + i, h)),
                  pl.BlockSpec((seq, dh), lambda b, h, i: (b, h)),
                  pl.BlockSpec((seq, dh), lambda b, h, i: (b, h))],
        out_specs=pl.BlockSpec((tq, dh), lambda b, h, i: (b * nq + i, h)),
        out_shape=jax.ShapeDtypeStruct((t, wq), BF16),
        scratch_shapes=[pltpu.VMEM((tq, dh), F32), pltpu.VMEM((tq, 1), F32),
                        pltpu.VMEM((SUBLANES, LANES), F32)],
        compiler_params=_cparams(("parallel", "parallel", "arbitrary")), name="sb_attn",
    )(q, k, v)


def _gla_kernel(q_ref, k_ref, gk_ref, v_ref, r_ref, gain_ref, o_ref, s_ref, *, ls, dk, dv):
    @pl.when(pl.program_id(1) == 0)
    def _():
        s_ref[...] = jnp.zeros_like(s_ref)

    gain = gain_ref[...]
    tri = _chunk_tri(ls)
    g1, g2, g3 = _split3(gk_ref[...])
    bc_all = _dot(tri, g1) + _dot(tri, g2) + _dot(tri, g3)
    ii = lax.broadcasted_iota(jnp.int32, (CHUNK, CHUNK), 0)
    jj = lax.broadcasted_iota(jnp.int32, (CHUNK, CHUNK), 1)
    causal = jj <= ii
    mid = CHUNK // 2
    for c in range(ls // CHUNK):
        rs = slice(c * CHUNK, (c + 1) * CHUNK)
        for h in range(GLA_HEADS):
            ks = slice(h * dk, (h + 1) * dk)
            vs = slice(h * dv, (h + 1) * dv)
            bc = bc_all[rs, ks]
            q = q_ref[rs, ks] * (dk ** -0.5)
            k = k_ref[rs, ks]
            v = v_ref[rs, vs]
            bmid = bc[mid:mid + 1, :]
            blast = bc[CHUNK - 1:CHUNK, :]
            att = _dot_nt((q * jnp.exp(bc - bmid)).astype(BF16), (k * jnp.exp(bmid - bc)).astype(BF16))
            att = jnp.where(causal, att, 0.0)
            s = s_ref[h]
            o = _dot(att.astype(BF16), v) + _dot((q * jnp.exp(bc)).astype(BF16), s.astype(BF16))
            kdec_t = (k * jnp.exp(blast - bc)).T
            eb_col = jnp.exp(bc[CHUNK - SUBLANES:CHUNK, :]).T[:, SUBLANES - 1:SUBLANES]
            s_ref[h] = eb_col * s + _dot(kdec_t.astype(BF16), v)
            rr = r_ref[rs, vs]
            o_ref[rs, vs] = (_rms(o, gain) * _silu(rr)).astype(BF16)


def _gla(q, k, gk, v, r, gain, batch, seq, ls):
    t, wk = q.shape
    wv = v.shape[1]
    dk = wk // GLA_HEADS
    dv = wv // GLA_HEADS
    ns = seq // ls
    tok = lambda wd: pl.BlockSpec((ls, wd), lambda b, s: (b * ns + s, 0))
    return pl.pallas_call(
        functools.partial(_gla_kernel, ls=ls, dk=dk, dv=dv),
        grid=(batch, ns),
        in_specs=[tok(wk), tok(wk), tok(wk), tok(wv), tok(wv),
                  pl.BlockSpec((1, dv), lambda b, s: (0, 0))],
        out_specs=tok(wv),
        out_shape=jax.ShapeDtypeStruct((t, wv), BF16),
        scratch_shapes=[pltpu.VMEM((GLA_HEADS, dk, dv), F32)],
        compiler_params=_cparams(("parallel", "arbitrary")), name="gla",
    )(q, k, gk, v, r, gain)


def _xa_kernel(x_ref, g_ref, wq_ref, gq_ref, k_ref, v_ref, wo_ref, o_ref, *, dh):
    x = x_ref[...]
    hn = _rms(x, g_ref[...]).astype(BF16)
    gq = gq_ref[...]
    acc = x
    for h in range(XA_HEADS):
        hs = slice(h * dh, (h + 1) * dh)
        qh = _rms(_dot(hn, wq_ref[:, hs]), gq).astype(BF16)
        sc = _dot_nt(qh, k_ref[:, hs]) * (dh ** -0.5)
        e = jnp.exp(sc - jnp.max(sc, axis=-1, keepdims=True))
        pr = e / jnp.sum(e, axis=-1, keepdims=True)
        oh = _dot(pr.astype(BF16), v_ref[:, hs])
        acc = acc + _dot(oh.astype(BF16), wo_ref[hs, :])
    o_ref[...] = acc


def _xattn(x, g, wq, gq, kmem, vmem, wo, seq, tm):
    t, d = x.shape
    dh = d // XA_HEADS
    nm = kmem.shape[0] // (t // seq)
    per = seq // tm
    return pl.pallas_call(
        functools.partial(_xa_kernel, dh=dh),
        grid=(t // tm,),
        in_specs=[pl.BlockSpec((tm, d), lambda i: (i, 0)),
                  pl.BlockSpec((1, d), lambda i: (0, 0)),
                  pl.BlockSpec((d, d), lambda i: (0, 0)),
                  pl.BlockSpec((1, dh), lambda i: (0, 0)),
                  pl.BlockSpec((nm, d), lambda i: (i // per, 0)),
                  pl.BlockSpec((nm, d), lambda i: (i // per, 0)),
                  pl.BlockSpec((d, d), lambda i: (0, 0))],
        out_specs=pl.BlockSpec((tm, d), lambda i: (i, 0)),
        out_shape=jax.ShapeDtypeStruct((t, d), F32),
        compiler_params=_cparams(("parallel",)), name="xattn",
    )(x, g.reshape(1, d), wq, gq.reshape(1, dh), kmem, vmem, wo)


def _ffn_kernel(x_ref, g_ref, wup_ref, cw_ref, cb_ref, wd_ref, o_ref, act_ref, carry_ref,
                *, tm, seq, nj, tff):
    x = x_ref[...]
    hn = _rms(x, g_ref[...]).astype(BF16)
    first = (pl.program_id(0) * tm) % seq == 0
    kw = cw_ref.shape[1]

    def branch(idx):
        u = _dot(hn, wup_ref[idx])
        halo = jnp.where(first, 0.0, carry_ref[idx])
        carry_ref[idx] = u[tm - SUBLANES:tm, :]
        ext = jnp.concatenate([halo, u], axis=0)
        cw = cw_ref[idx]
        y = cb_ref[idx] + cw[kw - 1:kw, :] * u
        for s in range(1, kw):
            y = y + cw[kw - 1 - s:kw - s, :] * ext[SUBLANES - s:SUBLANES - s + tm, :]
        return y

    for j in range(nj):
        gate = branch(j)
        val = branch(nj + j)
        act_ref[:, j * tff:(j + 1) * tff] = (_silu(gate) * val).astype(BF16)
    o_ref[...] = x + _dot(act_ref[...], wd_ref[...])


def _ffn(x, g, w_up, conv_w, conv_b, w_down, seq, tm, tff):
    t, d = x.shape
    dff = w_down.shape[0]
    nj = dff // tff
    kw = conv_w.shape[0]
    wup3 = w_up.reshape(d, 2 * nj, tff).transpose(1, 0, 2)
    cw3 = conv_w.reshape(kw, 2 * nj, tff).transpose(1, 0, 2)
    cb3 = conv_b.reshape(2 * nj, 1, tff)
    resident = lambda shape: pl.BlockSpec(shape, lambda i: (0,) * len(shape), pipeline_mode=pl.Buffered(1))
    return pl.pallas_call(
        functools.partial(_ffn_kernel, tm=tm, seq=seq, nj=nj, tff=tff),
        grid=(t // tm,),
        in_specs=[pl.BlockSpec((tm, d), lambda i: (i, 0)),
                  resident((1, d)),
                  resident(wup3.shape),
                  resident(cw3.shape),
                  resident(cb3.shape),
                  resident(w_down.shape)],
        out_specs=pl.BlockSpec((tm, d), lambda i: (i, 0)),
        out_shape=jax.ShapeDtypeStruct((t, d), F32),
        scratch_shapes=[pltpu.VMEM((tm, dff), BF16),
                        pltpu.VMEM((2 * nj, SUBLANES, tff), F32)],
        compiler_params=_cparams(("arbitrary",)), name="convffn",
    )(x, g.reshape(1, d), wup3, cw3, cb3, w_down)


def _pad_cols(w, n):
    return jnp.pad(w, ((0, 0), (0, n - w.shape[1])))


def _mixer_ab(x, g, w_in, conv_w, a_log, dt_bias, dn_g, sb_gq, sb_gk, w_out, batch, seq):
    dk = dn_g.shape[0]
    wq = DN_HEADS * dk
    dh = sb_gq.shape[0]
    wb = SB_HEADS * dh
    c_ab = 4 * wq
    w = jnp.concatenate([w_in[:, :c_ab], _pad_cols(w_in[:, c_ab:c_ab + 2 * DN_HEADS], LANES),
                         w_in[:, c_ab + 2 * DN_HEADS:]], axis=1).astype(BF16)
    o_qb = c_ab + LANES
    plan = [(0, 3 * wq, "raw", 0), (3 * wq, wq, "raw", 0), (c_ab, LANES, "raw", 0),
            (o_qb, wb, "headnorm", dh), (o_qb + wb, wb, "headnorm", dh), (o_qb + 2 * wb, wb, "raw", 0)]
    outs = [(3 * wq, F32), (wq, F32), (LANES, F32), (wb, BF16), (wb, BF16), (wb, BF16)]
    qkv_a, z_a, ab, q_b, k_b, v_b = _proj(
        x, g, w, [sb_gq.reshape(1, dh), sb_gk.reshape(1, dh)], plan, outs, tm=512)

    alog_row = _pad_cols(a_log.reshape(1, -1), LANES)
    dtb_row = _pad_cols(dt_bias.reshape(1, -1), LANES)
    u, wmat, qg, qk, kdt, egl = _dn_prep(qkv_a, ab, conv_w, alog_row, dtb_row, seq, lp=256)
    o_a = _dn_scan(u, wmat, qg, qk, kdt, egl, z_a, dn_g.reshape(1, dk), batch, seq, ls=512)
    o_b = _sb_attention(q_b, k_b, v_b, batch, seq, tq=256)
    wo = w_out.astype(BF16)
    return _oproj(x, [o_a, o_b], [wo[:wq], wo[wq:]], tm=512)


def _mixer_gla(x, g, w_in, w_gk, b_gk, norm_g, w_out, batch, seq):
    rank, wk = w_gk.shape
    dv = norm_g.shape[0]
    wv = GLA_HEADS * dv
    w = _pad_cols(w_in, 2 * wk + 2 * wv + LANES).astype(BF16)
    wgk = jnp.pad(w_gk, ((0, LANES - rank), (0, 0))).astype(BF16)
    plan = [(0, wk, "raw", 0), (wk, wk, "raw", 0), (2 * wk, wv, "raw", 0),
            (2 * wk + wv, wv, "raw", 0), (2 * wk + 2 * wv, LANES, "gk", 0)]
    outs = [(wk, F32), (wk, F32), (wv, BF16), (wv, F32), (wk, F32)]
    q, k, v, r, gk = _proj(x, g, w, [wgk, b_gk.reshape(1, wk)], plan, outs, tm=512)
    o = _gla(q, k, gk, v, r, norm_g.reshape(1, dv), batch, seq, ls=256)
    return _oproj(x, [o], [w_out.astype(BF16)], tm=512)


def _mem_kv(mem, g, w_kv, g_k):
    d = mem.shape[1]
    dh = g_k.shape[0]
    plan = [(0, d, "headnorm", dh), (d, d, "raw", 0)]
    outs = [(d, BF16), (d, BF16)]
    return _proj(mem, g, w_kv.astype(BF16), [g_k.reshape(1, dh)], plan, outs, tm=256)


def kernel(x, mem, norm_mix, norm_xa, norm_mem, norm_ffn, xa_w_q, xa_w_kv, xa_w_o, xa_g_q, xa_g_k, ffn_w_up, ffn_conv_w, ffn_conv_b, ffn_w_down, ab_w_in, ab_conv_w, dn_a_log, dn_dt_bias, dn_norm_g, sb_g_q, sb_g_k, ab_w_out, gla_w_in, gla_w_gk, gla_b_gk, gla_norm_g, gla_w_out):
    batch, seq, d = x.shape
    depth = norm_mix.shape[0]
    xf = x.reshape(batch * seq, d)
    memf = mem.reshape(batch * mem.shape[1], d)
    for layer in range(depth):
        i = layer // 2
        if layer % 2 == 0:
            xf = _mixer_ab(xf, norm_mix[layer], ab_w_in[i], ab_conv_w[i], dn_a_log[i], dn_dt_bias[i],
                           dn_norm_g[i], sb_g_q[i], sb_g_k[i], ab_w_out[i], batch, seq)
        else:
            xf = _mixer_gla(xf, norm_mix[layer], gla_w_in[i], gla_w_gk[i], gla_b_gk[i],
                            gla_norm_g[i], gla_w_out[i], batch, seq)
        kmem, vmem = _mem_kv(memf, norm_mem[layer], xa_w_kv[layer], xa_g_k[layer])
        xf = _xattn(xf, norm_xa[layer], xa_w_q[layer].astype(BF16), xa_g_q[layer], kmem, vmem,
                    xa_w_o[layer].astype(BF16), seq, tm=512)
        xf = _ffn(xf, norm_ffn[layer], ffn_w_up[layer].astype(BF16), ffn_conv_w[layer],
                  ffn_conv_b[layer], ffn_w_down[layer].astype(BF16), seq, tm=512, tff=256)
    return xf.reshape(batch, seq, d)
```

```python
import functools

import jax
import jax.numpy as jnp
from jax import lax
from jax.experimental import pallas as pl
from jax.experimental.pallas import tpu as pltpu

F32 = jnp.float32
BF16 = jnp.bfloat16
EPS = 1e-6

DN_HEADS = 4
SB_HEADS = 4
GLA_HEADS = 4
XA_HEADS = 4
GLA_TAU = 16.0
CHUNK = 64
LANES = 128
SUBLANES = 8
VMEM_LIMIT = 56 * 1024 * 1024
SB_UNDERFLOW = 110.0
SB_ROW_BLOCKS = 2


def _cparams(sem):
    return pltpu.CompilerParams(dimension_semantics=sem, vmem_limit_bytes=VMEM_LIMIT)


def _rms(x, g):
    ms = jnp.mean(x * x, axis=-1, keepdims=True)
    return x * lax.rsqrt(ms + EPS) * g


def _l2n(x):
    return x * lax.rsqrt(jnp.sum(x * x, axis=-1, keepdims=True) + EPS)


def _dot(a, b):
    return jnp.dot(a, b, preferred_element_type=F32)


def _dot_nt(a, b):
    return lax.dot_general(a, b, (((1,), (1,)), ((), ())), preferred_element_type=F32)


def _split2(x):
    hi = x.astype(BF16)
    lo = (x - hi.astype(F32)).astype(BF16)
    return hi, lo


def _split3(x):
    hi = x.astype(BF16)
    r = x - hi.astype(F32)
    mid = r.astype(BF16)
    lo = (r - mid.astype(F32)).astype(BF16)
    return hi, mid, lo


def _sigmoid(x):
    return 1.0 / (1.0 + jnp.exp(-x))


def _silu(x):
    return x * _sigmoid(x)


def _softplus(x):
    return jnp.maximum(x, 0.0) + jnp.log1p(jnp.exp(-jnp.abs(x)))


def _chunk_tri(n):
    r = lax.broadcasted_iota(jnp.int32, (n, n), 0)
    c = lax.broadcasted_iota(jnp.int32, (n, n), 1)
    same = (r // CHUNK) == (c // CHUNK)
    return jnp.where(same & (c <= r), 1.0, 0.0).astype(BF16)


def _proj_kernel(*refs, plan, n_aux):
    x_ref, g_ref, w_ref = refs[:3]
    aux = refs[3:3 + n_aux]
    outs = refs[3 + n_aux:]
    hn = _rms(x_ref[...], g_ref[...]).astype(BF16)
    ai = 0
    for (c0, width, mode, hd), o_ref in zip(plan, outs):
        if mode == "headnorm":
            gain = aux[ai][...]
            ai += 1
        elif mode == "gk":
            wgk_ref, bgk_ref = aux[ai], aux[ai + 1]
            ai += 2
        step = min(width, 512)
        for s0 in range(0, width, step):
            p = _dot(hn, w_ref[:, c0 + s0:c0 + s0 + step])
            if mode == "raw":
                o_ref[:, s0:s0 + step] = p.astype(o_ref.dtype)
            elif mode == "headnorm":
                for h0 in range(0, step, hd):
                    o_ref[:, s0 + h0:s0 + h0 + hd] = _rms(p[:, h0:h0 + hd], gain).astype(o_ref.dtype)
            else:
                y = _dot(p.astype(BF16), wgk_ref[...]) + bgk_ref[...]
                o_ref[...] = (jnp.minimum(y, 0.0) - jnp.log1p(jnp.exp(-jnp.abs(y)))) / GLA_TAU


def _proj(x, g, w, aux, plan, out_defs, tm):
    t, d = x.shape
    n = w.shape[1]
    in_specs = [pl.BlockSpec((tm, d), lambda i: (i, 0)),
                pl.BlockSpec((1, d), lambda i: (0, 0)),
                pl.BlockSpec((d, n), lambda i: (0, 0))]
    in_specs += [pl.BlockSpec(a.shape, lambda i: (0, 0)) for a in aux]
    out_shape = [jax.ShapeDtypeStruct((t, wd), dt) for wd, dt in out_defs]
    out_specs = [pl.BlockSpec((tm, wd), lambda i: (i, 0)) for wd, _ in out_defs]
    return pl.pallas_call(
        functools.partial(_proj_kernel, plan=tuple(plan), n_aux=len(aux)),
        grid=(t // tm,), in_specs=in_specs, out_specs=out_specs, out_shape=out_shape,
        compiler_params=_cparams(("parallel",)), name="proj",
    )(x, g.reshape(1, d), w, *aux)


def _dn_prep_kernel(qkv_ref, halo_ref, ab_ref, cw_ref, alog_ref, dtb_ref,
                    u_ref, w_ref, qg_ref, qk_ref, kdt_ref, egl_ref, buf_ref, *, lp, seq, dk):
    nh = DN_HEADS
    wq = nh * dk
    first = (pl.program_id(0) * lp) % seq == 0
    buf_ref[0:SUBLANES, :] = jnp.where(first, 0.0, halo_ref[...])
    buf_ref[SUBLANES:SUBLANES + lp, :] = qkv_ref[...]

    ab = ab_ref[...]
    g = -jnp.exp(alog_ref[...]) * _softplus(ab + dtb_ref[...])
    beta = _sigmoid(ab)
    tri = _chunk_tri(lp)
    g1, g2, g3 = _split3(g)
    gc = _dot(tri, g1) + _dot(tri, g2) + _dot(tri, g3)
    gct = gc.T
    nchunk = lp // CHUNK
    gl_rows = jnp.concatenate(
        [jnp.broadcast_to(gc[(c + 1) * CHUNK - 1:(c + 1) * CHUNK, :], (CHUNK, LANES))
         for c in range(nchunk)], axis=0)
    for c in range(nchunk):
        egl_ref[c] = jnp.exp(gc[(c + 1) * CHUNK - 1:(c + 1) * CHUNK, :])

    cw = cw_ref[...]
    kw = cw.shape[0]

    def convsilu(c0):
        y = None
        for i in range(kw):
            r0 = SUBLANES - (kw - 1) + i
            term = cw[i:i + 1, c0:c0 + dk] * buf_ref[r0:r0 + lp, c0:c0 + dk]
            y = term if y is None else y + term
        return _silu(y)

    pb = 2 * CHUNK
    pairs = range(lp // pb)
    ii = lax.broadcasted_iota(jnp.int32, (pb, pb), 0)
    jj = lax.broadcasted_iota(jnp.int32, (pb, pb), 1)
    same = (ii // CHUNK) == (jj // CHUNK)
    causal = same & (jj <= ii)
    strict = same & (jj < ii)
    eye = jnp.where(ii == jj, 1.0, 0.0)

    a_mats, vbs, kbgs = [], [], []
    for h in range(nh):
        cs = slice(h * dk, (h + 1) * dk)
        qn = _l2n(convsilu(h * dk)) * (dk ** -0.5)
        kn = _l2n(convsilu(wq + h * dk))
        v = convsilu(2 * wq + h * dk)
        gcol = gc[:, h:h + 1]
        bcol = beta[:, nh + h:nh + h + 1]
        egc = jnp.exp(gcol)
        kb = kn * bcol
        vbs.append(v * bcol)
        kbgs.append((kb * egc).astype(BF16))
        qg_ref[:, cs] = (qn * egc).astype(BF16)
        kdec = kn * jnp.exp(gl_rows[:, h:h + 1] - gcol)
        kdt = kdec.T.astype(BF16)
        for c in range(nchunk):
            kdt_ref[c, :, h * CHUNK:(h + 1) * CHUNK] = kdt[:, c * CHUNK:(c + 1) * CHUNK]
        knb = kn.astype(BF16)
        kbb = kb.astype(BF16)
        qnb = qn.astype(BF16)
        for p in pairs:
            pr = slice(p * pb, (p + 1) * pb)
            diff = gcol[pr] - gct[h:h + 1, pr]
            decay = jnp.where(causal, jnp.exp(jnp.where(causal, diff, 0.0)), 0.0)
            a_mats.append(jnp.where(strict, _dot_nt(kbb[pr], knb[pr]) * decay, 0.0))
            qk = (_dot_nt(qnb[pr], knb[pr]) * decay).astype(BF16)
            for half in range(2):
                sub = slice(half * CHUNK, (half + 1) * CHUNK)
                r0 = p * pb + half * CHUNK
                qk_ref[r0:r0 + CHUNK, h * CHUNK:(h + 1) * CHUNK] = qk[sub, sub]

    x_invs = [eye - a for a in a_mats]
    ps = a_mats
    for _ in range(5):
        pbs = [p.astype(BF16) for p in ps]
        ps = [_dot(b, b) for b in pbs]
        x_invs = [x + _dot(x.astype(BF16), p.astype(BF16)) for x, p in zip(x_invs, ps)]
    for h in range(nh):
        cs = slice(h * dk, (h + 1) * dk)
        for p in pairs:
            pr = slice(p * pb, (p + 1) * pb)
            x1, x2 = _split2(x_invs[h * len(pairs) + p])
            v1, v2 = _split2(vbs[h][pr])
            u_ref[pr, cs] = _dot(x1, v1) + _dot(x1, v2) + _dot(x2, v1)
            w_ref[pr, cs] = _dot(x1, kbgs[h][pr]).astype(BF16)


def _dn_prep(qkv, ab, conv_w, alog_row, dtb_row, seq, lp):
    t, wqkv = qkv.shape
    dk = wqkv // (3 * DN_HEADS)
    wq = DN_HEADS * dk
    hb = lp // SUBLANES
    nchunk = lp // CHUNK
    out_shape = [jax.ShapeDtypeStruct((t, wq), F32),
                 jax.ShapeDtypeStruct((t, wq), BF16),
                 jax.ShapeDtypeStruct((t, wq), BF16),
                 jax.ShapeDtypeStruct((t, DN_HEADS * CHUNK), BF16),
                 jax.ShapeDtypeStruct((t // CHUNK, dk, DN_HEADS * CHUNK), BF16),
                 jax.ShapeDtypeStruct((t // CHUNK, 1, LANES), F32)]
    out_specs = [pl.BlockSpec((lp, wq), lambda i: (i, 0)),
                 pl.BlockSpec((lp, wq), lambda i: (i, 0)),
                 pl.BlockSpec((lp, wq), lambda i: (i, 0)),
                 pl.BlockSpec((lp, DN_HEADS * CHUNK), lambda i: (i, 0)),
                 pl.BlockSpec((nchunk, dk, DN_HEADS * CHUNK), lambda i: (i, 0, 0)),
                 pl.BlockSpec((nchunk, 1, LANES), lambda i: (i, 0, 0))]
    in_specs = [pl.BlockSpec((lp, wqkv), lambda i: (i, 0)),
                pl.BlockSpec((SUBLANES, wqkv), lambda i: (jnp.maximum(i * hb - 1, 0), 0)),
                pl.BlockSpec((lp, LANES), lambda i: (i, 0)),
                pl.BlockSpec(conv_w.shape, lambda i: (0, 0)),
                pl.BlockSpec((1, LANES), lambda i: (0, 0)),
                pl.BlockSpec((1, LANES), lambda i: (0, 0))]
    return pl.pallas_call(
        functools.partial(_dn_prep_kernel, lp=lp, seq=seq, dk=dk),
        grid=(t // lp,), in_specs=in_specs, out_specs=out_specs, out_shape=out_shape,
        scratch_shapes=[pltpu.VMEM((lp + SUBLANES, wqkv), F32)],
        compiler_params=_cparams(("parallel",)), name="dn_prep",
    )(qkv, qkv, ab, conv_w, alog_row, dtb_row)


def _dn_scan_kernel(u_ref, w_ref, qg_ref, qk_ref, kdt_ref, egl_ref, z_ref, gain_ref,
                    o_ref, s_ref, *, ls, dk):
    @pl.when(pl.program_id(1) == 0)
    def _():
        s_ref[...] = jnp.zeros_like(s_ref)

    gain = gain_ref[...]

    def body(c, carry):
        r0 = pl.multiple_of(c * CHUNK, CHUNK)
        rows = pl.ds(r0, CHUNK)
        e_full = jnp.broadcast_to(egl_ref[c], (dk, LANES))
        heads = range(DN_HEADS)
        cs = [slice(h * dk, (h + 1) * dk) for h in heads]
        hs = [slice(h * CHUNK, (h + 1) * CHUNK) for h in heads]
        ss = [s_ref[h] for h in heads]
        sbs = [s.astype(BF16) for s in ss]
        wss = [_dot(w_ref[rows, cs[h]], sbs[h]) for h in heads]
        oqs = [_dot(qg_ref[rows, cs[h]], sbs[h]) for h in heads]
        dbs = [(u_ref[rows, cs[h]] - wss[h]).astype(BF16) for h in heads]
        ods = [_dot(qk_ref[rows, hs[h]], dbs[h]) for h in heads]
        sds = [_dot(kdt_ref[c, :, hs[h]], dbs[h]) for h in heads]
        for h in heads:
            s_ref[h] = e_full[:, h:h + 1] * ss[h] + sds[h]
        for h in heads:
            o = oqs[h] + ods[h]
            o_ref[rows, cs[h]] = (_rms(o, gain) * _silu(z_ref[rows, cs[h]])).astype(BF16)
        return carry

    lax.fori_loop(0, ls // CHUNK, body, 0)


def _dn_scan(u, w, qg, qk, kdt, egl, z, gain, batch, seq, ls):
    t, wq = u.shape
    dk = wq // DN_HEADS
    ns = seq // ls
    nchunk = ls // CHUNK
    tok = lambda wd: pl.BlockSpec((ls, wd), lambda b, s: (b * ns + s, 0))
    in_specs = [tok(wq), tok(wq), tok(wq), tok(DN_HEADS * CHUNK),
                pl.BlockSpec((nchunk, dk, DN_HEADS * CHUNK), lambda b, s: (b * ns + s, 0, 0)),
                pl.BlockSpec((nchunk, 1, LANES), lambda b, s: (b * ns + s, 0, 0)),
                tok(wq),
                pl.BlockSpec((1, dk), lambda b, s: (0, 0))]
    return pl.pallas_call(
        functools.partial(_dn_scan_kernel, ls=ls, dk=dk),
        grid=(batch, ns), in_specs=in_specs, out_specs=tok(wq),
        out_shape=jax.ShapeDtypeStruct((t, wq), BF16),
        scratch_shapes=[pltpu.VMEM((DN_HEADS, dk, dk), F32)],
        compiler_params=_cparams(("parallel", "arbitrary")), name="dn_scan",
    )(u, w, qg, qk, kdt, egl, z, gain)


def _sb_kernel(q_ref, k_ref, v_ref, o_ref, acc_ref, r_ref, kmax_ref, *, tq, nblk, scale):
    qi = pl.program_id(2)
    rb = tq // nblk

    @pl.when(qi == 0)
    def _():
        kf = k_ref[...].astype(F32)
        kn2 = jnp.sum(kf * kf, axis=-1, keepdims=True)
        kmax_ref[...] = jnp.broadcast_to(jnp.sqrt(jnp.max(kn2, axis=0, keepdims=True)), kmax_ref.shape)

    qf = q_ref[...].astype(F32)
    zb = scale * jnp.sqrt(jnp.sum(qf * qf, axis=-1, keepdims=True)) * kmax_ref[0:1, 0:1]
    rr = lax.broadcasted_iota(jnp.int32, (tq, tq), 0)
    cc = lax.broadcasted_iota(jnp.int32, (tq, tq), 1)
    suffix = jnp.where(rr >= cc, 1.0, 0.0).astype(BF16)

    acc_ref[...] = jnp.zeros_like(acc_ref)
    r_ref[...] = jnp.zeros_like(r_ref)

    def tile(kt, masked):
        k0 = pl.multiple_of(kt * tq, tq)
        kk = k_ref[pl.ds(k0, tq), :]
        vv = v_ref[pl.ds(k0, tq), :]
        blocks = range(nblk)
        rs = [slice(b * rb, (b + 1) * rb) for b in blocks]
        zs = [_dot_nt(q_ref[rs[b], :], kk) * scale for b in blocks]
        ls = [-(jnp.maximum(z, 0.0) + jnp.log(1.0 + jnp.exp(-jnp.abs(z)))) for z in zs]
        if masked:
            kpos = lax.broadcasted_iota(jnp.int32, (rb, tq), 1)
            strict = [kpos < lax.broadcasted_iota(jnp.int32, (rb, tq), 0) + b * rb for b in blocks]
            ls = [jnp.where(strict[b], ls[b], 0.0) for b in blocks]
        parts = [_split2(l) for l in ls]
        csums = [_dot(hi, suffix) + _dot(lo, suffix) for hi, lo in parts]
        atts = [jnp.exp(zs[b] + csums[b] + r_ref[rs[b], :]) for b in blocks]
        if masked:
            atts = [jnp.where(strict[b], atts[b], 0.0) for b in blocks]
        pvs = [_dot(att.astype(BF16), vv) for att in atts]
        for b in blocks:
            acc_ref[rs[b], :] += pvs[b]
            r_ref[rs[b], :] += csums[b][:, 0:1]

    def live():
        return jnp.max(r_ref[...] + zb)

    tile(qi, True)

    def cond(carry):
        j, m = carry
        return jnp.logical_and(j < qi, m > -SB_UNDERFLOW)

    def body(carry):
        j, _ = carry
        tile(qi - 1 - j, False)
        return j + 1, live()

    lax.while_loop(cond, body, (jnp.int32(0), live()))
    o_ref[...] = acc_ref[...].astype(BF16)


def _sb_attention(q, k, v, batch, seq, tq):
    t, wq = q.shape
    dh = wq // SB_HEADS
    nq = seq // tq
    return pl.pallas_call(
        functools.partial(_sb_kernel, tq=tq, nblk=SB_ROW_BLOCKS, scale=dh ** -0.5),
        grid=(batch, SB_HEADS, nq),
        in_specs=[pl.BlockSpec((tq, dh), lambda b, h, i: (b * nq + i, h)),
                  pl.BlockSpec((seq, dh), lambda b, h, i: (b, h)),
                  pl.BlockSpec((seq, dh), lambda b, h, i: (b, h))],
        out_specs=pl.BlockSpec((tq, dh), lambda b, h, i: (b * nq + i, h)),
        out_shape=jax.ShapeDtypeStruct((t, wq), BF16),
        scratch_shapes=[pltpu.VMEM((tq, dh), F32), pltpu.VMEM((tq, 1), F32),
                        pltpu.VMEM((SUBLANES, LANES), F32)],
        compiler_params=_cparams(("parallel", "parallel", "arbitrary")), name="sb_attn",
    )(q, k, v)


def _gla_kernel(q_ref, k_ref, gk_ref, v_ref, r_ref, gain_ref, o_ref, s_ref, *, ls, dk, dv):
    @pl.when(pl.program_id(1) == 0)
    def _():
        s_ref[...] = jnp.zeros_like(s_ref)

    gain = gain_ref[...]
    tri = _chunk_tri(ls)
    g1, g2, g3 = _split3(gk_ref[...])
    bc_all = _dot(tri, g1) + _dot(tri, g2) + _dot(tri, g3)
    ii = lax.broadcasted_iota(jnp.int32, (ls, ls), 0)
    jj = lax.broadcasted_iota(jnp.int32, (ls, ls), 1)
    causal = ((ii // CHUNK) == (jj // CHUNK)) & (jj <= ii)
    chunks = range(ls // CHUNK)
    heads = range(GLA_HEADS)
    rs = [slice(c * CHUNK, (c + 1) * CHUNK) for c in chunks]
    ks = [slice(h * dk, (h + 1) * dk) for h in heads]
    vs = [slice(h * dv, (h + 1) * dv) for h in heads]

    def chunk_row(x, r):
        return jnp.concatenate(
            [jnp.broadcast_to(x[c * CHUNK + r:c * CHUNK + r + 1, :], (CHUNK, x.shape[1])) for c in chunks], axis=0)

    bcs = [bc_all[:, ks[h]] for h in heads]
    bmids = [chunk_row(bc, CHUNK // 2) for bc in bcs]
    blasts = [chunk_row(bc, CHUNK - 1) for bc in bcs]
    qs = [q_ref[:, ks[h]] * (dk ** -0.5) for h in heads]
    kk = [k_ref[:, ks[h]] for h in heads]
    vv = [v_ref[:, vs[h]] for h in heads]
    atts = [_dot_nt((qs[h] * jnp.exp(bcs[h] - bmids[h])).astype(BF16),
                    (kk[h] * jnp.exp(bmids[h] - bcs[h])).astype(BF16)) for h in heads]
    atts = [jnp.where(causal, a, 0.0).astype(BF16) for a in atts]
    o_intra = [_dot(atts[h], vv[h]) for h in heads]
    kdts = [(kk[h] * jnp.exp(blasts[h] - bcs[h])).T.astype(BF16) for h in heads]
    bcts = [bc.T for bc in bcs]
    qgs = [(qs[h] * jnp.exp(bcs[h])).astype(BF16) for h in heads]
    upd = [[_dot(kdts[h][:, rs[c]], vv[h][rs[c]]) for c in chunks] for h in heads]
    snaps = []
    for h in heads:
        s = s_ref[h]
        sn = []
        for c in chunks:
            sn.append(s.astype(BF16))
            last = (c + 1) * CHUNK - 1
            s = jnp.exp(bcts[h][:, last:last + 1]) * s + upd[h][c]
        s_ref[h] = s
        snaps.append(sn)
    for h in heads:
        o_inter = jnp.concatenate([_dot(qgs[h][rs[c]], snaps[h][c]) for c in chunks], axis=0)
        o = o_intra[h] + o_inter
        o_ref[:, vs[h]] = (_rms(o, gain) * _silu(r_ref[:, vs[h]])).astype(BF16)


def _gla(q, k, gk, v, r, gain, batch, seq, ls):
    t, wk = q.shape
    wv = v.shape[1]
    dk = wk // GLA_HEADS
    dv = wv // GLA_HEADS
    ns = seq // ls
    tok = lambda wd: pl.BlockSpec((ls, wd), lambda b, s: (b * ns + s, 0))
    return pl.pallas_call(
        functools.partial(_gla_kernel, ls=ls, dk=dk, dv=dv),
        grid=(batch, ns),
        in_specs=[tok(wk), tok(wk), tok(wk), tok(wv), tok(wv),
                  pl.BlockSpec((1, dv), lambda b, s: (0, 0))],
        out_specs=tok(wv),
        out_shape=jax.ShapeDtypeStruct((t, wv), BF16),
        scratch_shapes=[pltpu.VMEM((GLA_HEADS, dk, dv), F32)],
        compiler_params=_cparams(("parallel", "arbitrary")), name="gla",
    )(q, k, gk, v, r, gain)


def _post_kernel(*refs, n_act, tm, seq, nj, tff, dh):
    x_ref = refs[0]
    a_refs = refs[1:1 + n_act]
    wout_refs = refs[1 + n_act:1 + 2 * n_act]
    (gxa_ref, wq_ref, gq_ref, k_ref, v_ref, wo_ref,
     gff_ref, wup_ref, cw_ref, cb_ref, wd_ref, o_ref, act_ref, carry_ref) = refs[1 + 2 * n_act:]

    x = x_ref[...]
    for a_ref, w_ref in zip(a_refs, wout_refs):
        x = x + _dot(a_ref[...], w_ref[...])

    hn = _rms(x, gxa_ref[...]).astype(BF16)
    gq = gq_ref[...]
    heads = range(XA_HEADS)
    hs = [slice(h * dh, (h + 1) * dh) for h in heads]
    qs = [_dot(hn, wq_ref[:, hs[h]]) for h in heads]
    qn = [_rms(q, gq).astype(BF16) for q in qs]
    sc = [_dot_nt(qn[h], k_ref[:, hs[h]]) * (dh ** -0.5) for h in heads]
    es = [jnp.exp(s - jnp.max(s, axis=-1, keepdims=True)) for s in sc]
    pr = [(e / jnp.sum(e, axis=-1, keepdims=True)).astype(BF16) for e in es]
    oh = [_dot(pr[h], v_ref[:, hs[h]]).astype(BF16) for h in heads]
    x = x + _dot(jnp.concatenate(oh, axis=1), wo_ref[...])

    hn = _rms(x, gff_ref[...]).astype(BF16)
    first = (pl.program_id(0) * tm) % seq == 0
    kw = cw_ref.shape[1]

    def branch(idx):
        u = _dot(hn, wup_ref[idx])
        halo = jnp.where(first, 0.0, carry_ref[idx])
        carry_ref[idx] = u[tm - SUBLANES:tm, :]
        ext = jnp.concatenate([halo, u], axis=0)
        cw = cw_ref[idx]
        y = cb_ref[idx] + cw[kw - 1:kw, :] * u
        for s in range(1, kw):
            y = y + cw[kw - 1 - s:kw - s, :] * ext[SUBLANES - s:SUBLANES - s + tm, :]
        return y

    for j in range(nj):
        gate = branch(j)
        val = branch(nj + j)
        act_ref[:, j * tff:(j + 1) * tff] = (_silu(gate) * val).astype(BF16)
    o_ref[...] = x + _dot(act_ref[...], wd_ref[...])


def _post(x, acts, wouts, g_xa, wq, gq, kmem, vmem, wo, g_ffn, w_up, conv_w, conv_b, w_down, seq, tm, tff):
    t, d = x.shape
    dh = gq.shape[0]
    nm = kmem.shape[0] // (t // seq)
    per = seq // tm
    dff = w_down.shape[0]
    nj = dff // tff
    kw = conv_w.shape[0]
    wup3 = w_up.reshape(d, 2 * nj, tff).transpose(1, 0, 2)
    cw3 = conv_w.reshape(kw, 2 * nj, tff).transpose(1, 0, 2)
    cb3 = conv_b.reshape(2 * nj, 1, tff)
    resident = lambda shape: pl.BlockSpec(shape, lambda i: (0,) * len(shape), pipeline_mode=pl.Buffered(1))
    tok = lambda wd: pl.BlockSpec((tm, wd), lambda i: (i, 0))
    in_specs = [tok(d)] + [tok(a.shape[1]) for a in acts] + [resident(w.shape) for w in wouts]
    in_specs += [resident((1, d)), resident((d, d)), resident((1, dh)),
                 pl.BlockSpec((nm, d), lambda i: (i // per, 0)),
                 pl.BlockSpec((nm, d), lambda i: (i // per, 0)),
                 resident((d, d)),
                 resident((1, d)), resident(wup3.shape), resident(cw3.shape), resident(cb3.shape),
                 resident(w_down.shape)]
    return pl.pallas_call(
        functools.partial(_post_kernel, n_act=len(acts), tm=tm, seq=seq, nj=nj, tff=tff, dh=dh),
        grid=(t // tm,), in_specs=in_specs, out_specs=tok(d),
        out_shape=jax.ShapeDtypeStruct((t, d), F32),
        scratch_shapes=[pltpu.VMEM((tm, dff), BF16),
                        pltpu.VMEM((2 * nj, SUBLANES, tff), F32)],
        compiler_params=_cparams(("arbitrary",)), name="post",
    )(x, *acts, *wouts, g_xa.reshape(1, d), wq, gq.reshape(1, dh), kmem, vmem, wo,
      g_ffn.reshape(1, d), wup3, cw3, cb3, w_down)


def _pad_cols(w, n):
    return jnp.pad(w, ((0, 0), (0, n - w.shape[1])))


def _mixer_ab(x, g, w_in, conv_w, a_log, dt_bias, dn_g, sb_gq, sb_gk, w_out, batch, seq):
    dk = dn_g.shape[0]
    wq = DN_HEADS * dk
    dh = sb_gq.shape[0]
    wb = SB_HEADS * dh
    c_ab = 4 * wq
    w = jnp.concatenate([w_in[:, :c_ab], _pad_cols(w_in[:, c_ab:c_ab + 2 * DN_HEADS], LANES),
                         w_in[:, c_ab + 2 * DN_HEADS:]], axis=1).astype(BF16)
    o_qb = c_ab + LANES
    plan = [(0, 3 * wq, "raw", 0), (3 * wq, wq, "raw", 0), (c_ab, LANES, "raw", 0),
            (o_qb, wb, "headnorm", dh), (o_qb + wb, wb, "headnorm", dh), (o_qb + 2 * wb, wb, "raw", 0)]
    outs = [(3 * wq, F32), (wq, F32), (LANES, F32), (wb, BF16), (wb, BF16), (wb, BF16)]
    qkv_a, z_a, ab, q_b, k_b, v_b = _proj(
        x, g, w, [sb_gq.reshape(1, dh), sb_gk.reshape(1, dh)], plan, outs, tm=512)

    alog_row = _pad_cols(a_log.reshape(1, -1), LANES)
    dtb_row = _pad_cols(dt_bias.reshape(1, -1), LANES)
    u, wmat, qg, qk, kdt, egl = _dn_prep(qkv_a, ab, conv_w, alog_row, dtb_row, seq, lp=256)
    o_a = _dn_scan(u, wmat, qg, qk, kdt, egl, z_a, dn_g.reshape(1, dk), batch, seq, ls=512)
    o_b = _sb_attention(q_b, k_b, v_b, batch, seq, tq=256)
    wo = w_out.astype(BF16)
    return [o_a, o_b], [wo[:wq], wo[wq:]]


def _mixer_gla(x, g, w_in, w_gk, b_gk, norm_g, w_out, batch, seq):
    rank, wk = w_gk.shape
    dv = norm_g.shape[0]
    wv = GLA_HEADS * dv
    w = _pad_cols(w_in, 2 * wk + 2 * wv + LANES).astype(BF16)
    wgk = jnp.pad(w_gk, ((0, LANES - rank), (0, 0))).astype(BF16)
    plan = [(0, wk, "raw", 0), (wk, wk, "raw", 0), (2 * wk, wv, "raw", 0),
            (2 * wk + wv, wv, "raw", 0), (2 * wk + 2 * wv, LANES, "gk", 0)]
    outs = [(wk, F32), (wk, F32), (wv, BF16), (wv, F32), (wk, F32)]
    q, k, v, r, gk = _proj(x, g, w, [wgk, b_gk.reshape(1, wk)], plan, outs, tm=512)
    o = _gla(q, k, gk, v, r, norm_g.reshape(1, dv), batch, seq, ls=256)
    return [o], [w_out.astype(BF16)]


def _mem_kv(mem, g, w_kv, g_k):
    d = mem.shape[1]
    dh = g_k.shape[0]
    plan = [(0, d, "headnorm", dh), (d, d, "raw", 0)]
    outs = [(d, BF16), (d, BF16)]
    return _proj(mem, g, w_kv.astype(BF16), [g_k.reshape(1, dh)], plan, outs, tm=256)


def kernel(x, mem, norm_mix, norm_xa, norm_mem, norm_ffn, xa_w_q, xa_w_kv, xa_w_o, xa_g_q, xa_g_k, ffn_w_up, ffn_conv_w, ffn_conv_b, ffn_w_down, ab_w_in, ab_conv_w, dn_a_log, dn_dt_bias, dn_norm_g, sb_g_q, sb_g_k, ab_w_out, gla_w_in, gla_w_gk, gla_b_gk, gla_norm_g, gla_w_out):
    batch, seq, d = x.shape
    depth = norm_mix.shape[0]
    xf = x.reshape(batch * seq, d)
    memf = mem.reshape(batch * mem.shape[1], d)
    for layer in range(depth):
        i = layer // 2
        if layer % 2 == 0:
            acts, wouts = _mixer_ab(xf, norm_mix[layer], ab_w_in[i], ab_conv_w[i], dn_a_log[i], dn_dt_bias[i],
                                    dn_norm_g[i], sb_g_q[i], sb_g_k[i], ab_w_out[i], batch, seq)
        else:
            acts, wouts = _mixer_gla(xf, norm_mix[layer], gla_w_in[i], gla_w_gk[i], gla_b_gk[i],
                                     gla_norm_g[i], gla_w_out[i], batch, seq)
        kmem, vmem = _mem_kv(memf, norm_mem[layer], xa_w_kv[layer], xa_g_k[layer])
        xf = _post(xf, acts, wouts, norm_xa[layer], xa_w_q[layer].astype(BF16), xa_g_q[layer], kmem, vmem,
                   xa_w_o[layer].astype(BF16), norm_ffn[layer], ffn_w_up[layer].astype(BF16),
                   ffn_conv_w[layer], ffn_conv_b[layer], ffn_w_down[layer].astype(BF16), seq, tm=512, tff=256)
    return xf.reshape(batch, seq, d)
```

```python
import functools

import jax
import jax.numpy as jnp
from jax import lax
from jax.experimental import pallas as pl
from jax.experimental.pallas import tpu as pltpu

F32 = jnp.float32
BF16 = jnp.bfloat16
EPS = 1e-6

DN_HEADS = 4
SB_HEADS = 4
GLA_HEADS = 4
XA_HEADS = 4
GLA_TAU = 16.0
CHUNK = 64
LANES = 128
SUBLANES = 8
VMEM_LIMIT = 56 * 1024 * 1024
SB_UNDERFLOW = 110.0
SB_ROW_BLOCKS = 2


def _cparams(sem):
    return pltpu.CompilerParams(dimension_semantics=sem, vmem_limit_bytes=VMEM_LIMIT)


def _rms(x, g):
    ms = jnp.mean(x * x, axis=-1, keepdims=True)
    return x * lax.rsqrt(ms + EPS) * g


def _l2n(x):
    return x * lax.rsqrt(jnp.sum(x * x, axis=-1, keepdims=True) + EPS)


def _dot(a, b):
    return jnp.dot(a, b, preferred_element_type=F32)


def _dot_nt(a, b):
    return lax.dot_general(a, b, (((1,), (1,)), ((), ())), preferred_element_type=F32)


def _split2(x):
    hi = x.astype(BF16)
    lo = (x - hi.astype(F32)).astype(BF16)
    return hi, lo


def _split3(x):
    hi = x.astype(BF16)
    r = x - hi.astype(F32)
    mid = r.astype(BF16)
    lo = (r - mid.astype(F32)).astype(BF16)
    return hi, mid, lo


def _sigmoid(x):
    return 1.0 / (1.0 + jnp.exp(-x))


def _silu(x):
    return x * _sigmoid(x)


def _softplus(x):
    return jnp.maximum(x, 0.0) + jnp.log1p(jnp.exp(-jnp.abs(x)))


def _chunk_tri(n):
    r = lax.broadcasted_iota(jnp.int32, (n, n), 0)
    c = lax.broadcasted_iota(jnp.int32, (n, n), 1)
    same = (r // CHUNK) == (c // CHUNK)
    return jnp.where(same & (c <= r), 1.0, 0.0).astype(BF16)


def _proj_kernel(*refs, plan, n_aux):
    x_ref, g_ref, w_ref = refs[:3]
    aux = refs[3:3 + n_aux]
    outs = refs[3 + n_aux:]
    hn = _rms(x_ref[...], g_ref[...]).astype(BF16)
    ai = 0
    for (c0, width, mode, hd), o_ref in zip(plan, outs):
        if mode == "headnorm":
            gain = aux[ai][...]
            ai += 1
        elif mode == "gk":
            wgk_ref, bgk_ref = aux[ai], aux[ai + 1]
            ai += 2
        step = min(width, 512)
        for s0 in range(0, width, step):
            p = _dot(hn, w_ref[:, c0 + s0:c0 + s0 + step])
            if mode == "raw":
                o_ref[:, s0:s0 + step] = p.astype(o_ref.dtype)
            elif mode == "headnorm":
                for h0 in range(0, step, hd):
                    o_ref[:, s0 + h0:s0 + h0 + hd] = _rms(p[:, h0:h0 + hd], gain).astype(o_ref.dtype)
            else:
                y = _dot(p.astype(BF16), wgk_ref[...]) + bgk_ref[...]
                o_ref[...] = (jnp.minimum(y, 0.0) - jnp.log1p(jnp.exp(-jnp.abs(y)))) / GLA_TAU


def _proj(x, g, w, aux, plan, out_defs, tm):
    t, d = x.shape
    n = w.shape[1]
    in_specs = [pl.BlockSpec((tm, d), lambda i: (i, 0)),
                pl.BlockSpec((1, d), lambda i: (0, 0)),
                pl.BlockSpec((d, n), lambda i: (0, 0))]
    in_specs += [pl.BlockSpec(a.shape, lambda i: (0, 0)) for a in aux]
    out_shape = [jax.ShapeDtypeStruct((t, wd), dt) for wd, dt in out_defs]
    out_specs = [pl.BlockSpec((tm, wd), lambda i: (i, 0)) for wd, _ in out_defs]
    return pl.pallas_call(
        functools.partial(_proj_kernel, plan=tuple(plan), n_aux=len(aux)),
        grid=(t // tm,), in_specs=in_specs, out_specs=out_specs, out_shape=out_shape,
        compiler_params=_cparams(("parallel",)), name="proj",
    )(x, g.reshape(1, d), w, *aux)


def _dn_prep_kernel(qkv_ref, halo_ref, ab_ref, cw_ref, alog_ref, dtb_ref,
                    u_ref, w_ref, qg_ref, qk_ref, kdt_ref, egl_ref, buf_ref, *, lp, seq, dk):
    nh = DN_HEADS
    wq = nh * dk
    first = (pl.program_id(0) * lp) % seq == 0
    buf_ref[0:SUBLANES, :] = jnp.where(first, 0.0, halo_ref[...])
    buf_ref[SUBLANES:SUBLANES + lp, :] = qkv_ref[...]

    ab = ab_ref[...]
    g = -jnp.exp(alog_ref[...]) * _softplus(ab + dtb_ref[...])
    beta = _sigmoid(ab)
    tri = _chunk_tri(lp)
    g1, g2, g3 = _split3(g)
    gc = _dot(tri, g1) + _dot(tri, g2) + _dot(tri, g3)
    gct = gc.T
    nchunk = lp // CHUNK
    gl_rows = jnp.concatenate(
        [jnp.broadcast_to(gc[(c + 1) * CHUNK - 1:(c + 1) * CHUNK, :], (CHUNK, LANES))
         for c in range(nchunk)], axis=0)
    for c in range(nchunk):
        egl_ref[c] = jnp.exp(gc[(c + 1) * CHUNK - 1:(c + 1) * CHUNK, :])

    cw = cw_ref[...]
    kw = cw.shape[0]

    def convsilu(c0):
        y = None
        for i in range(kw):
            r0 = SUBLANES - (kw - 1) + i
            term = cw[i:i + 1, c0:c0 + dk] * buf_ref[r0:r0 + lp, c0:c0 + dk]
            y = term if y is None else y + term
        return _silu(y)

    pb = 2 * CHUNK
    pairs = range(lp // pb)
    ii = lax.broadcasted_iota(jnp.int32, (pb, pb), 0)
    jj = lax.broadcasted_iota(jnp.int32, (pb, pb), 1)
    same = (ii // CHUNK) == (jj // CHUNK)
    causal = same & (jj <= ii)
    strict = same & (jj < ii)
    eye = jnp.where(ii == jj, 1.0, 0.0)

    a_mats, vbs, kbgs = [], [], []
    for h in range(nh):
        cs = slice(h * dk, (h + 1) * dk)
        qn = _l2n(convsilu(h * dk)) * (dk ** -0.5)
        kn = _l2n(convsilu(wq + h * dk))
        v = convsilu(2 * wq + h * dk)
        gcol = gc[:, h:h + 1]
        bcol = beta[:, nh + h:nh + h + 1]
        egc = jnp.exp(gcol)
        kb = kn * bcol
        vbs.append(v * bcol)
        kbgs.append((kb * egc).astype(BF16))
        qg_ref[:, cs] = (qn * egc).astype(BF16)
        kdec = kn * jnp.exp(gl_rows[:, h:h + 1] - gcol)
        kdt = kdec.T.astype(BF16)
        for c in range(nchunk):
            kdt_ref[c, :, h * CHUNK:(h + 1) * CHUNK] = kdt[:, c * CHUNK:(c + 1) * CHUNK]
        knb = kn.astype(BF16)
        kbb = kb.astype(BF16)
        qnb = qn.astype(BF16)
        for p in pairs:
            pr = slice(p * pb, (p + 1) * pb)
            diff = gcol[pr] - gct[h:h + 1, pr]
            decay = jnp.where(causal, jnp.exp(jnp.where(causal, diff, 0.0)), 0.0)
            a_mats.append(jnp.where(strict, _dot_nt(kbb[pr], knb[pr]) * decay, 0.0))
            qk = (_dot_nt(qnb[pr], knb[pr]) * decay).astype(BF16)
            for half in range(2):
                sub = slice(half * CHUNK, (half + 1) * CHUNK)
                r0 = p * pb + half * CHUNK
                qk_ref[r0:r0 + CHUNK, h * CHUNK:(h + 1) * CHUNK] = qk[sub, sub]

    x_invs = [eye - a for a in a_mats]
    ps = a_mats
    for _ in range(5):
        pbs = [p.astype(BF16) for p in ps]
        ps = [_dot(b, b) for b in pbs]
        x_invs = [x + _dot(x.astype(BF16), p.astype(BF16)) for x, p in zip(x_invs, ps)]
    for h in range(nh):
        cs = slice(h * dk, (h + 1) * dk)
        for p in pairs:
            pr = slice(p * pb, (p + 1) * pb)
            x1, x2 = _split2(x_invs[h * len(pairs) + p])
            v1, v2 = _split2(vbs[h][pr])
            u_ref[pr, cs] = _dot(x1, v1) + _dot(x1, v2) + _dot(x2, v1)
            w_ref[pr, cs] = _dot(x1, kbgs[h][pr]).astype(BF16)


def _dn_prep(qkv, ab, conv_w, alog_row, dtb_row, seq, lp):
    t, wqkv = qkv.shape
    dk = wqkv // (3 * DN_HEADS)
    wq = DN_HEADS * dk
    hb = lp // SUBLANES
    nchunk = lp // CHUNK
    out_shape = [jax.ShapeDtypeStruct((t, wq), F32),
                 jax.ShapeDtypeStruct((t, wq), BF16),
                 jax.ShapeDtypeStruct((t, wq), BF16),
                 jax.ShapeDtypeStruct((t, DN_HEADS * CHUNK), BF16),
                 jax.ShapeDtypeStruct((t // CHUNK, dk, DN_HEADS * CHUNK), BF16),
                 jax.ShapeDtypeStruct((t // CHUNK, 1, LANES), F32)]
    out_specs = [pl.BlockSpec((lp, wq), lambda i: (i, 0)),
                 pl.BlockSpec((lp, wq), lambda i: (i, 0)),
                 pl.BlockSpec((lp, wq), lambda i: (i, 0)),
                 pl.BlockSpec((lp, DN_HEADS * CHUNK), lambda i: (i, 0)),
                 pl.BlockSpec((nchunk, dk, DN_HEADS * CHUNK), lambda i: (i, 0, 0)),
                 pl.BlockSpec((nchunk, 1, LANES), lambda i: (i, 0, 0))]
    in_specs = [pl.BlockSpec((lp, wqkv), lambda i: (i, 0)),
                pl.BlockSpec((SUBLANES, wqkv), lambda i: (jnp.maximum(i * hb - 1, 0), 0)),
                pl.BlockSpec((lp, LANES), lambda i: (i, 0)),
                pl.BlockSpec(conv_w.shape, lambda i: (0, 0)),
                pl.BlockSpec((1, LANES), lambda i: (0, 0)),
                pl.BlockSpec((1, LANES), lambda i: (0, 0))]
    return pl.pallas_call(
        functools.partial(_dn_prep_kernel, lp=lp, seq=seq, dk=dk),
        grid=(t // lp,), in_specs=in_specs, out_specs=out_specs, out_shape=out_shape,
        scratch_shapes=[pltpu.VMEM((lp + SUBLANES, wqkv), F32)],
        compiler_params=_cparams(("parallel",)), name="dn_prep",
    )(qkv, qkv, ab, conv_w, alog_row, dtb_row)


def _dn_scan_kernel(u_ref, w_ref, qg_ref, qk_ref, kdt_ref, egl_ref, z_ref, gain_ref,
                    o_ref, s_ref, *, ls, dk, nb):
    @pl.when(pl.program_id(1) == 0)
    def _():
        s_ref[...] = jnp.zeros_like(s_ref)

    gain = gain_ref[...]
    heads = range(DN_HEADS)
    cs = [slice(h * dk, (h + 1) * dk) for h in heads]
    hs = [slice(h * CHUNK, (h + 1) * CHUNK) for h in heads]
    items = [(b, h) for b in range(nb) for h in heads]

    def body(c, carry):
        r0 = pl.multiple_of(c * CHUNK, CHUNK)
        rows = pl.ds(r0, CHUNK)
        e_full = [jnp.broadcast_to(egl_ref[b, c], (dk, LANES)) for b in range(nb)]
        ss = [s_ref[b * DN_HEADS + h] for b, h in items]
        sbs = [s.astype(BF16) for s in ss]
        wss = [_dot(w_ref[b, rows, cs[h]], sbs[i]) for i, (b, h) in enumerate(items)]
        oqs = [_dot(qg_ref[b, rows, cs[h]], sbs[i]) for i, (b, h) in enumerate(items)]
        dbs = [(u_ref[b, rows, cs[h]] - wss[i]).astype(BF16) for i, (b, h) in enumerate(items)]
        ods = [_dot(qk_ref[b, rows, hs[h]], dbs[i]) for i, (b, h) in enumerate(items)]
        sds = [_dot(kdt_ref[b, c, :, hs[h]], dbs[i]) for i, (b, h) in enumerate(items)]
        for i, (b, h) in enumerate(items):
            s_ref[b * DN_HEADS + h] = e_full[b][:, h:h + 1] * ss[i] + sds[i]
        for i, (b, h) in enumerate(items):
            o = oqs[i] + ods[i]
            o_ref[b, rows, cs[h]] = (_rms(o, gain) * _silu(z_ref[b, rows, cs[h]])).astype(BF16)
        return carry

    lax.fori_loop(0, ls // CHUNK, body, 0)


def _dn_scan(u, w, qg, qk, kdt, egl, z, gain, batch, seq, ls, nb):
    t, wq = u.shape
    dk = wq // DN_HEADS
    ns = seq // ls
    nchunk = ls // CHUNK
    seq3 = lambda a: a.reshape(batch, seq, a.shape[-1])
    tok = lambda wd: pl.BlockSpec((nb, ls, wd), lambda b, s: (b, s, 0))
    in_specs = [tok(wq), tok(wq), tok(wq), tok(DN_HEADS * CHUNK),
                pl.BlockSpec((nb, nchunk, dk, DN_HEADS * CHUNK), lambda b, s: (b, s, 0, 0)),
                pl.BlockSpec((nb, nchunk, 1, LANES), lambda b, s: (b, s, 0, 0)),
                tok(wq),
                pl.BlockSpec((1, dk), lambda b, s: (0, 0))]
    out = pl.pallas_call(
        functools.partial(_dn_scan_kernel, ls=ls, dk=dk, nb=nb),
        grid=(batch // nb, ns), in_specs=in_specs, out_specs=tok(wq),
        out_shape=jax.ShapeDtypeStruct((batch, seq, wq), BF16),
        scratch_shapes=[pltpu.VMEM((nb * DN_HEADS, dk, dk), F32)],
        compiler_params=_cparams(("parallel", "arbitrary")), name="dn_scan",
    )(seq3(u), seq3(w), seq3(qg), seq3(qk),
      kdt.reshape(batch, seq // CHUNK, dk, DN_HEADS * CHUNK), egl.reshape(batch, seq // CHUNK, 1, LANES),
      seq3(z), gain)
    return out.reshape(t, wq)


def _sb_kernel(q_ref, k_ref, v_ref, o_ref, acc_ref, r_ref, kmax_ref, *, tq, nblk, scale):
    qi = pl.program_id(2)
    rb = tq // nblk

    @pl.when(qi == 0)
    def _():
        kf = k_ref[...].astype(F32)
        kn2 = jnp.sum(kf * kf, axis=-1, keepdims=True)
        kmax_ref[...] = jnp.broadcast_to(jnp.sqrt(jnp.max(kn2, axis=0, keepdims=True)), kmax_ref.shape)

    qf = q_ref[...].astype(F32)
    zb = scale * jnp.sqrt(jnp.sum(qf * qf, axis=-1, keepdims=True)) * kmax_ref[0:1, 0:1]
    rr = lax.broadcasted_iota(jnp.int32, (tq, tq), 0)
    cc = lax.broadcasted_iota(jnp.int32, (tq, tq), 1)
    suffix = jnp.where(rr >= cc, 1.0, 0.0).astype(BF16)

    acc_ref[...] = jnp.zeros_like(acc_ref)
    r_ref[...] = jnp.zeros_like(r_ref)

    blocks = range(nblk)
    rs = [slice(b * rb, (b + 1) * rb) for b in blocks]
    kpos = lax.broadcasted_iota(jnp.int32, (rb, tq), 1)
    strict = [kpos < lax.broadcasted_iota(jnp.int32, (rb, tq), 0) + b * rb for b in blocks]

    def tiles(specs):
        items = [(t, b) for t in range(len(specs)) for b in blocks]
        k0 = [pl.multiple_of(kt * tq, tq) for kt, _, _ in specs]
        kk = [k_ref[pl.ds(k, tq), :] for k in k0]
        vv = [v_ref[pl.ds(k, tq), :] for k in k0]
        zs = {(t, b): _dot_nt(q_ref[rs[b], :], kk[t]) * scale for t, b in items}
        ls = {it: -(jnp.maximum(z, 0.0) + jnp.log(1.0 + jnp.exp(-jnp.abs(z)))) for it, z in zs.items()}
        for t, b in items:
            if specs[t][1]:
                ls[t, b] = jnp.where(strict[b], ls[t, b], 0.0)
        parts = {it: _split2(l) for it, l in ls.items()}
        csums = {it: _dot(hi, suffix) + _dot(lo, suffix) for it, (hi, lo) in parts.items()}
        carry = [r_ref[rs[b], :] for b in blocks]
        atts = {}
        for t, b in items:
            att = jnp.exp(zs[t, b] + csums[t, b] + carry[b])
            if specs[t][1]:
                att = jnp.where(strict[b], att, 0.0)
            if specs[t][2] is not None:
                att = jnp.where(specs[t][2], att, 0.0)
            atts[t, b] = att.astype(BF16)
            carry[b] = carry[b] + csums[t, b][:, 0:1]
        pvs = {(t, b): _dot(atts[t, b], vv[t]) for t, b in items}
        for b in blocks:
            acc = acc_ref[rs[b], :]
            for t in range(len(specs)):
                acc = acc + pvs[t, b]
            acc_ref[rs[b], :] = acc
            r_ref[rs[b], :] = carry[b]

    def live():
        return jnp.max(r_ref[...] + zb)

    tiles([(qi, True, None), (jnp.maximum(qi - 1, 0), False, qi > 0)])

    def cond(carry):
        j, m = carry
        return jnp.logical_and(j < qi, m > -SB_UNDERFLOW)

    def body(carry):
        j, _ = carry
        tiles([(qi - 1 - j, False, None)])
        return j + 1, live()

    lax.while_loop(cond, body, (jnp.int32(1), live()))
    o_ref[...] = acc_ref[...].astype(BF16)


def _sb_attention(q, k, v, batch, seq, tq):
    t, wq = q.shape
    dh = wq // SB_HEADS
    nq = seq // tq
    return pl.pallas_call(
        functools.partial(_sb_kernel, tq=tq, nblk=SB_ROW_BLOCKS, scale=dh ** -0.5),
        grid=(batch, SB_HEADS, nq),
        in_specs=[pl.BlockSpec((tq, dh), lambda b, h, i: (b * nq + i, h)),
                  pl.BlockSpec((seq, dh), lambda b, h, i: (b, h)),
                  pl.BlockSpec((seq, dh), lambda b, h, i: (b, h))],
        out_specs=pl.BlockSpec((tq, dh), lambda b, h, i: (b * nq + i, h)),
        out_shape=jax.ShapeDtypeStruct((t, wq), BF16),
        scratch_shapes=[pltpu.VMEM((tq, dh), F32), pltpu.VMEM((tq, 1), F32),
                        pltpu.VMEM((SUBLANES, LANES), F32)],
        compiler_params=_cparams(("parallel", "parallel", "arbitrary")), name="sb_attn",
    )(q, k, v)


def _gla_kernel(q_ref, k_ref, gk_ref, v_ref, r_ref, gain_ref, o_ref, s_ref, *, ls, dk, dv):
    @pl.when(pl.program_id(1) == 0)
    def _():
        s_ref[...] = jnp.zeros_like(s_ref)

    gain = gain_ref[...]
    tri = _chunk_tri(ls)
    g1, g2, g3 = _split3(gk_ref[...])
    bc_all = _dot(tri, g1) + _dot(tri, g2) + _dot(tri, g3)
    ii = lax.broadcasted_iota(jnp.int32, (ls, ls), 0)
    jj = lax.broadcasted_iota(jnp.int32, (ls, ls), 1)
    causal = ((ii // CHUNK) == (jj // CHUNK)) & (jj <= ii)
    chunks = range(ls // CHUNK)
    heads = range(GLA_HEADS)
    rs = [slice(c * CHUNK, (c + 1) * CHUNK) for c in chunks]
    ks = [slice(h * dk, (h + 1) * dk) for h in heads]
    vs = [slice(h * dv, (h + 1) * dv) for h in heads]

    def chunk_row(x, r):
        return jnp.concatenate(
            [jnp.broadcast_to(x[c * CHUNK + r:c * CHUNK + r + 1, :], (CHUNK, x.shape[1])) for c in chunks], axis=0)

    bcs = [bc_all[:, ks[h]] for h in heads]
    bmids = [chunk_row(bc, CHUNK // 2) for bc in bcs]
    blasts = [chunk_row(bc, CHUNK - 1) for bc in bcs]
    qs = [q_ref[:, ks[h]] * (dk ** -0.5) for h in heads]
    kk = [k_ref[:, ks[h]] for h in heads]
    vv = [v_ref[:, vs[h]] for h in heads]
    atts = [_dot_nt((qs[h] * jnp.exp(bcs[h] - bmids[h])).astype(BF16),
                    (kk[h] * jnp.exp(bmids[h] - bcs[h])).astype(BF16)) for h in heads]
    atts = [jnp.where(causal, a, 0.0).astype(BF16) for a in atts]
    o_intra = [_dot(atts[h], vv[h]) for h in heads]
    kdts = [(kk[h] * jnp.exp(blasts[h] - bcs[h])).T.astype(BF16) for h in heads]
    bcts = [bc.T for bc in bcs]
    qgs = [(qs[h] * jnp.exp(bcs[h])).astype(BF16) for h in heads]
    upd = [[_dot(kdts[h][:, rs[c]], vv[h][rs[c]]) for c in chunks] for h in heads]
    snaps = []
    for h in heads:
        s = s_ref[h]
        sn = []
        for c in chunks:
            sn.append(s.astype(BF16))
            last = (c + 1) * CHUNK - 1
            s = jnp.exp(bcts[h][:, last:last + 1]) * s + upd[h][c]
        s_ref[h] = s
        snaps.append(sn)
    for h in heads:
        o_inter = jnp.concatenate([_dot(qgs[h][rs[c]], snaps[h][c]) for c in chunks], axis=0)
        o = o_intra[h] + o_inter
        o_ref[:, vs[h]] = (_rms(o, gain) * _silu(r_ref[:, vs[h]])).astype(BF16)


def _gla(q, k, gk, v, r, gain, batch, seq, ls):
    t, wk = q.shape
    wv = v.shape[1]
    dk = wk // GLA_HEADS
    dv = wv // GLA_HEADS
    ns = seq // ls
    tok = lambda wd: pl.BlockSpec((ls, wd), lambda b, s: (b * ns + s, 0))
    return pl.pallas_call(
        functools.partial(_gla_kernel, ls=ls, dk=dk, dv=dv),
        grid=(batch, ns),
        in_specs=[tok(wk), tok(wk), tok(wk), tok(wv), tok(wv),
                  pl.BlockSpec((1, dv), lambda b, s: (0, 0))],
        out_specs=tok(wv),
        out_shape=jax.ShapeDtypeStruct((t, wv), BF16),
        scratch_shapes=[pltpu.VMEM((GLA_HEADS, dk, dv), F32)],
        compiler_params=_cparams(("parallel", "arbitrary")), name="gla",
    )(q, k, gk, v, r, gain)


def _post_kernel(*refs, n_act, tm, seq, nj, tff, dh):
    x_ref = refs[0]
    a_refs = refs[1:1 + n_act]
    wout_refs = refs[1 + n_act:1 + 2 * n_act]
    (gxa_ref, wq_ref, gq_ref, k_ref, v_ref, wo_ref,
     gff_ref, wup_ref, cw_ref, cb_ref, wd_ref, o_ref, act_ref, carry_ref) = refs[1 + 2 * n_act:]

    x = x_ref[...]
    for a_ref, w_ref in zip(a_refs, wout_refs):
        x = x + _dot(a_ref[...], w_ref[...])

    hn = _rms(x, gxa_ref[...]).astype(BF16)
    gq = gq_ref[...]
    heads = range(XA_HEADS)
    hs = [slice(h * dh, (h + 1) * dh) for h in heads]
    qs = [_dot(hn, wq_ref[:, hs[h]]) for h in heads]
    qn = [_rms(q, gq).astype(BF16) for q in qs]
    sc = [_dot_nt(qn[h], k_ref[:, hs[h]]) * (dh ** -0.5) for h in heads]
    es = [jnp.exp(s - jnp.max(s, axis=-1, keepdims=True)) for s in sc]
    pr = [(e / jnp.sum(e, axis=-1, keepdims=True)).astype(BF16) for e in es]
    oh = [_dot(pr[h], v_ref[:, hs[h]]).astype(BF16) for h in heads]
    x = x + _dot(jnp.concatenate(oh, axis=1), wo_ref[...])

    hn = _rms(x, gff_ref[...]).astype(BF16)
    first = (pl.program_id(0) * tm) % seq == 0
    kw = cw_ref.shape[0]

    def branch(idx):
        cols = slice(idx * tff, (idx + 1) * tff)
        u = _dot(hn, wup_ref[:, cols])
        halo = jnp.where(first, 0.0, carry_ref[idx])
        carry_ref[idx] = u[tm - SUBLANES:tm, :]
        ext = jnp.concatenate([halo, u], axis=0)
        cw = cw_ref[:, cols]
        y = cb_ref[:, cols] + cw[kw - 1:kw, :] * u
        for s in range(1, kw):
            y = y + cw[kw - 1 - s:kw - s, :] * ext[SUBLANES - s:SUBLANES - s + tm, :]
        return y

    for j in range(nj):
        gate = branch(j)
        val = branch(nj + j)
        act_ref[:, j * tff:(j + 1) * tff] = (_silu(gate) * val).astype(BF16)
    o_ref[...] = x + _dot(act_ref[...], wd_ref[...])


def _post(x, acts, wouts, g_xa, wq, gq, kmem, vmem, wo, g_ffn, w_up, conv_w, conv_b, w_down, seq, tm, tff):
    t, d = x.shape
    dh = gq.shape[0]
    nm = kmem.shape[0] // (t // seq)
    per = seq // tm
    dff = w_down.shape[0]
    nj = dff // tff
    cb = conv_b.reshape(1, -1)
    resident = lambda shape: pl.BlockSpec(shape, lambda i: (0,) * len(shape), pipeline_mode=pl.Buffered(1))
    tok = lambda wd: pl.BlockSpec((tm, wd), lambda i: (i, 0))
    in_specs = [tok(d)] + [tok(a.shape[1]) for a in acts] + [resident(w.shape) for w in wouts]
    in_specs += [resident((1, d)), resident((d, d)), resident((1, dh)),
                 pl.BlockSpec((nm, d), lambda i: (i // per, 0)),
                 pl.BlockSpec((nm, d), lambda i: (i // per, 0)),
                 resident((d, d)),
                 resident((1, d)), resident(w_up.shape), resident(conv_w.shape), resident(cb.shape),
                 resident(w_down.shape)]
    return pl.pallas_call(
        functools.partial(_post_kernel, n_act=len(acts), tm=tm, seq=seq, nj=nj, tff=tff, dh=dh),
        grid=(t // tm,), in_specs=in_specs, out_specs=tok(d),
        out_shape=jax.ShapeDtypeStruct((t, d), F32),
        scratch_shapes=[pltpu.VMEM((tm, dff), BF16),
                        pltpu.VMEM((2 * nj, SUBLANES, tff), F32)],
        compiler_params=_cparams(("arbitrary",)), name="post",
    )(x, *acts, *wouts, g_xa.reshape(1, d), wq, gq.reshape(1, dh), kmem, vmem, wo,
      g_ffn.reshape(1, d), w_up, conv_w, cb, w_down)


def _pad_cols(w, n):
    return jnp.pad(w, ((0, 0), (0, n - w.shape[1])))


def _mixer_ab(x, g, w_in, conv_w, a_log, dt_bias, dn_g, sb_gq, sb_gk, w_out, batch, seq):
    dk = dn_g.shape[0]
    wq = DN_HEADS * dk
    dh = sb_gq.shape[0]
    wb = SB_HEADS * dh
    c_ab = 4 * wq
    w = jnp.concatenate([w_in[:, :c_ab], _pad_cols(w_in[:, c_ab:c_ab + 2 * DN_HEADS], LANES),
                         w_in[:, c_ab + 2 * DN_HEADS:]], axis=1).astype(BF16)
    o_qb = c_ab + LANES
    plan = [(0, 3 * wq, "raw", 0), (3 * wq, wq, "raw", 0), (c_ab, LANES, "raw", 0),
            (o_qb, wb, "headnorm", dh), (o_qb + wb, wb, "headnorm", dh), (o_qb + 2 * wb, wb, "raw", 0)]
    outs = [(3 * wq, F32), (wq, F32), (LANES, F32), (wb, BF16), (wb, BF16), (wb, BF16)]
    qkv_a, z_a, ab, q_b, k_b, v_b = _proj(
        x, g, w, [sb_gq.reshape(1, dh), sb_gk.reshape(1, dh)], plan, outs, tm=512)

    alog_row = _pad_cols(a_log.reshape(1, -1), LANES)
    dtb_row = _pad_cols(dt_bias.reshape(1, -1), LANES)
    u, wmat, qg, qk, kdt, egl = _dn_prep(qkv_a, ab, conv_w, alog_row, dtb_row, seq, lp=256)
    nb = 4 if batch % 4 == 0 else (2 if batch % 2 == 0 else 1)
    o_a = _dn_scan(u, wmat, qg, qk, kdt, egl, z_a, dn_g.reshape(1, dk), batch, seq, ls=256, nb=nb)
    o_b = _sb_attention(q_b, k_b, v_b, batch, seq, tq=256)
    wo = w_out.astype(BF16)
    return [o_a, o_b], [wo[:wq], wo[wq:]]


def _mixer_gla(x, g, w_in, w_gk, b_gk, norm_g, w_out, batch, seq):
    rank, wk = w_gk.shape
    dv = norm_g.shape[0]
    wv = GLA_HEADS * dv
    w = _pad_cols(w_in, 2 * wk + 2 * wv + LANES).astype(BF16)
    wgk = jnp.pad(w_gk, ((0, LANES - rank), (0, 0))).astype(BF16)
    plan = [(0, wk, "raw", 0), (wk, wk, "raw", 0), (2 * wk, wv, "raw", 0),
            (2 * wk + wv, wv, "raw", 0), (2 * wk + 2 * wv, LANES, "gk", 0)]
    outs = [(wk, F32), (wk, F32), (wv, BF16), (wv, F32), (wk, F32)]
    q, k, v, r, gk = _proj(x, g, w, [wgk, b_gk.reshape(1, wk)], plan, outs, tm=512)
    o = _gla(q, k, gk, v, r, norm_g.reshape(1, dv), batch, seq, ls=256)
    return [o], [w_out.astype(BF16)]


def _mem_kv(mem, g, w_kv, g_k):
    d = mem.shape[1]
    dh = g_k.shape[0]
    plan = [(0, d, "headnorm", dh), (d, d, "raw", 0)]
    outs = [(d, BF16), (d, BF16)]
    return _proj(mem, g, w_kv.astype(BF16), [g_k.reshape(1, dh)], plan, outs, tm=256)


def kernel(x, mem, norm_mix, norm_xa, norm_mem, norm_ffn, xa_w_q, xa_w_kv, xa_w_o, xa_g_q, xa_g_k, ffn_w_up, ffn_conv_w, ffn_conv_b, ffn_w_down, ab_w_in, ab_conv_w, dn_a_log, dn_dt_bias, dn_norm_g, sb_g_q, sb_g_k, ab_w_out, gla_w_in, gla_w_gk, gla_b_gk, gla_norm_g, gla_w_out):
    batch, seq, d = x.shape
    depth = norm_mix.shape[0]
    xf = x.reshape(batch * seq, d)
    memf = mem.reshape(batch * mem.shape[1], d)
    for layer in range(depth):
        i = layer // 2
        if layer % 2 == 0:
            acts, wouts = _mixer_ab(xf, norm_mix[layer], ab_w_in[i], ab_conv_w[i], dn_a_log[i], dn_dt_bias[i],
                                    dn_norm_g[i], sb_g_q[i], sb_g_k[i], ab_w_out[i], batch, seq)
        else:
            acts, wouts = _mixer_gla(xf, norm_mix[layer], gla_w_in[i], gla_w_gk[i], gla_b_gk[i],
                                     gla_norm_g[i], gla_w_out[i], batch, seq)
        kmem, vmem = _mem_kv(memf, norm_mem[layer], xa_w_kv[layer], xa_g_k[layer])
        xf = _post(xf, acts, wouts, norm_xa[layer], xa_w_q[layer].astype(BF16), xa_g_q[layer], kmem, vmem,
                   xa_w_o[layer].astype(BF16), norm_ffn[layer], ffn_w_up[layer].astype(BF16),
                   ffn_conv_w[layer], ffn_conv_b[layer], ffn_w_down[layer].astype(BF16), seq, tm=512, tff=256)
    return xf.reshape(batch, seq, d)
```

```python
import functools

import jax
import jax.numpy as jnp
from jax import lax
from jax.experimental import pallas as pl
from jax.experimental.pallas import tpu as pltpu

F32 = jnp.float32
BF16 = jnp.bfloat16
EPS = 1e-6

DN_HEADS = 4
SB_HEADS = 4
GLA_HEADS = 4
XA_HEADS = 4
GLA_TAU = 16.0
CHUNK = 64
LANES = 128
SUBLANES = 8
VMEM_LIMIT = 56 * 1024 * 1024
SB_UNDERFLOW = 110.0
SB_ROW_BLOCKS = 2
SB_HEADS_PER_STEP = 2


def _cparams(sem):
    return pltpu.CompilerParams(dimension_semantics=sem, vmem_limit_bytes=VMEM_LIMIT)


def _rms(x, g):
    ms = jnp.mean(x * x, axis=-1, keepdims=True)
    return x * lax.rsqrt(ms + EPS) * g


def _l2n(x):
    return x * lax.rsqrt(jnp.sum(x * x, axis=-1, keepdims=True) + EPS)


def _dot(a, b):
    return jnp.dot(a, b, preferred_element_type=F32)


def _dot_nt(a, b):
    return lax.dot_general(a, b, (((1,), (1,)), ((), ())), preferred_element_type=F32)


def _split2(x):
    hi = x.astype(BF16)
    lo = (x - hi.astype(F32)).astype(BF16)
    return hi, lo


def _split3(x):
    hi = x.astype(BF16)
    r = x - hi.astype(F32)
    mid = r.astype(BF16)
    lo = (r - mid.astype(F32)).astype(BF16)
    return hi, mid, lo


def _sigmoid(x):
    return 1.0 / (1.0 + jnp.exp(-x))


def _silu(x):
    return x * _sigmoid(x)


def _softplus(x):
    return jnp.maximum(x, 0.0) + jnp.log1p(jnp.exp(-jnp.abs(x)))


def _chunk_tri(n):
    r = lax.broadcasted_iota(jnp.int32, (n, n), 0)
    c = lax.broadcasted_iota(jnp.int32, (n, n), 1)
    same = (r // CHUNK) == (c // CHUNK)
    return jnp.where(same & (c <= r), 1.0, 0.0).astype(BF16)


def _proj_kernel(*refs, plan, n_aux):
    x_ref, g_ref, w_ref = refs[:3]
    aux = refs[3:3 + n_aux]
    outs = refs[3 + n_aux:]
    hn = _rms(x_ref[...], g_ref[...]).astype(BF16)
    ai = 0
    for (c0, width, mode, hd), o_ref in zip(plan, outs):
        if mode == "headnorm":
            gain = aux[ai][...]
            ai += 1
        elif mode == "gk":
            wgk_ref, bgk_ref = aux[ai], aux[ai + 1]
            ai += 2
        step = min(width, 512)
        for s0 in range(0, width, step):
            p = _dot(hn, w_ref[:, c0 + s0:c0 + s0 + step])
            if mode == "raw":
                o_ref[:, s0:s0 + step] = p.astype(o_ref.dtype)
            elif mode == "headnorm":
                for h0 in range(0, step, hd):
                    o_ref[:, s0 + h0:s0 + h0 + hd] = _rms(p[:, h0:h0 + hd], gain).astype(o_ref.dtype)
            else:
                y = _dot(p.astype(BF16), wgk_ref[...]) + bgk_ref[...]
                o_ref[...] = (jnp.minimum(y, 0.0) - jnp.log1p(jnp.exp(-jnp.abs(y)))) / GLA_TAU


def _proj(x, g, w, aux, plan, out_defs, tm):
    t, d = x.shape
    n = w.shape[1]
    in_specs = [pl.BlockSpec((tm, d), lambda i: (i, 0)),
                pl.BlockSpec((1, d), lambda i: (0, 0)),
                pl.BlockSpec((d, n), lambda i: (0, 0))]
    in_specs += [pl.BlockSpec(a.shape, lambda i: (0, 0)) for a in aux]
    out_shape = [jax.ShapeDtypeStruct((t, wd), dt) for wd, dt in out_defs]
    out_specs = [pl.BlockSpec((tm, wd), lambda i: (i, 0)) for wd, _ in out_defs]
    return pl.pallas_call(
        functools.partial(_proj_kernel, plan=tuple(plan), n_aux=len(aux)),
        grid=(t // tm,), in_specs=in_specs, out_specs=out_specs, out_shape=out_shape,
        compiler_params=_cparams(("parallel",)), name="proj",
    )(x, g.reshape(1, d), w, *aux)


def _dn_prep_kernel(qkv_ref, halo_ref, ab_ref, cw_ref, alog_ref, dtb_ref,
                    u_ref, w_ref, qg_ref, qk_ref, kdt_ref, egl_ref, buf_ref, *, lp, seq, dk):
    nh = DN_HEADS
    wq = nh * dk
    first = (pl.program_id(0) * lp) % seq == 0
    buf_ref[0:SUBLANES, :] = jnp.where(first, 0.0, halo_ref[...])
    buf_ref[SUBLANES:SUBLANES + lp, :] = qkv_ref[...]

    ab = ab_ref[...]
    g = -jnp.exp(alog_ref[...]) * _softplus(ab + dtb_ref[...])
    beta = _sigmoid(ab)
    tri = _chunk_tri(lp)
    g1, g2, g3 = _split3(g)
    gc = _dot(tri, g1) + _dot(tri, g2) + _dot(tri, g3)
    gct = gc.T
    nchunk = lp // CHUNK
    gl_rows = jnp.concatenate(
        [jnp.broadcast_to(gc[(c + 1) * CHUNK - 1:(c + 1) * CHUNK, :], (CHUNK, LANES))
         for c in range(nchunk)], axis=0)
    for c in range(nchunk):
        egl_ref[c] = jnp.exp(gc[(c + 1) * CHUNK - 1:(c + 1) * CHUNK, :])

    cw = cw_ref[...]
    kw = cw.shape[0]

    def convsilu(c0):
        cur = buf_ref[SUBLANES:SUBLANES + lp, c0:c0 + dk]
        halo = buf_ref[0:SUBLANES, c0:c0 + dk]
        y = cw[kw - 1:kw, c0:c0 + dk] * cur
        row = lax.broadcasted_iota(jnp.int32, (SUBLANES, dk), 0)
        for s in range(1, kw):
            shifted = pltpu.roll(cur, s, axis=0)
            head = jnp.where(row < s, pltpu.roll(halo, s, axis=0), shifted[:SUBLANES])
            shifted = jnp.concatenate([head, shifted[SUBLANES:]], axis=0)
            y = y + cw[kw - 1 - s:kw - s, c0:c0 + dk] * shifted
        return _silu(y)

    pb = 2 * CHUNK
    pairs = range(lp // pb)
    ii = lax.broadcasted_iota(jnp.int32, (pb, pb), 0)
    jj = lax.broadcasted_iota(jnp.int32, (pb, pb), 1)
    same = (ii // CHUNK) == (jj // CHUNK)
    causal = same & (jj <= ii)
    strict = same & (jj < ii)
    eye = jnp.where(ii == jj, 1.0, 0.0)

    a_mats, vbs, kbgs = [], [], []
    for h in range(nh):
        cs = slice(h * dk, (h + 1) * dk)
        qn = _l2n(convsilu(h * dk)) * (dk ** -0.5)
        kn = _l2n(convsilu(wq + h * dk))
        v = convsilu(2 * wq + h * dk)
        gcol = gc[:, h:h + 1]
        bcol = beta[:, nh + h:nh + h + 1]
        egc = jnp.exp(gcol)
        kb = kn * bcol
        vbs.append(v * bcol)
        kbgs.append((kb * egc).astype(BF16))
        qg_ref[:, cs] = (qn * egc).astype(BF16)
        kdec = kn * jnp.exp(gl_rows[:, h:h + 1] - gcol)
        kdt = kdec.T.astype(BF16)
        for c in range(nchunk):
            kdt_ref[c, :, h * CHUNK:(h + 1) * CHUNK] = kdt[:, c * CHUNK:(c + 1) * CHUNK]
        knb = kn.astype(BF16)
        kbb = kb.astype(BF16)
        qnb = qn.astype(BF16)
        for p in pairs:
            pr = slice(p * pb, (p + 1) * pb)
            diff = gcol[pr] - gct[h:h + 1, pr]
            decay = jnp.where(causal, jnp.exp(jnp.where(causal, diff, 0.0)), 0.0)
            a_mats.append(jnp.where(strict, _dot_nt(kbb[pr], knb[pr]) * decay, 0.0))
            qk = (_dot_nt(qnb[pr], knb[pr]) * decay).astype(BF16)
            for half in range(2):
                sub = slice(half * CHUNK, (half + 1) * CHUNK)
                r0 = p * pb + half * CHUNK
                qk_ref[r0:r0 + CHUNK, h * CHUNK:(h + 1) * CHUNK] = qk[sub, sub]

    x_invs = [eye - a for a in a_mats]
    ps = a_mats
    for _ in range(5):
        pbs = [p.astype(BF16) for p in ps]
        ps = [_dot(b, b) for b in pbs]
        x_invs = [x + _dot(x.astype(BF16), p.astype(BF16)) for x, p in zip(x_invs, ps)]
    for h in range(nh):
        cs = slice(h * dk, (h + 1) * dk)
        for p in pairs:
            pr = slice(p * pb, (p + 1) * pb)
            x1, x2 = _split2(x_invs[h * len(pairs) + p])
            v1, v2 = _split2(vbs[h][pr])
            u_ref[pr, cs] = _dot(x1, v1) + _dot(x1, v2) + _dot(x2, v1)
            w_ref[pr, cs] = _dot(x1, kbgs[h][pr]).astype(BF16)


def _dn_prep(qkv, ab, conv_w, alog_row, dtb_row, seq, lp):
    t, wqkv = qkv.shape
    dk = wqkv // (3 * DN_HEADS)
    wq = DN_HEADS * dk
    hb = lp // SUBLANES
    nchunk = lp // CHUNK
    out_shape = [jax.ShapeDtypeStruct((t, wq), F32),
                 jax.ShapeDtypeStruct((t, wq), BF16),
                 jax.ShapeDtypeStruct((t, wq), BF16),
                 jax.ShapeDtypeStruct((t, DN_HEADS * CHUNK), BF16),
                 jax.ShapeDtypeStruct((t // CHUNK, dk, DN_HEADS * CHUNK), BF16),
                 jax.ShapeDtypeStruct((t // CHUNK, 1, LANES), F32)]
    out_specs = [pl.BlockSpec((lp, wq), lambda i: (i, 0)),
                 pl.BlockSpec((lp, wq), lambda i: (i, 0)),
                 pl.BlockSpec((lp, wq), lambda i: (i, 0)),
                 pl.BlockSpec((lp, DN_HEADS * CHUNK), lambda i: (i, 0)),
                 pl.BlockSpec((nchunk, dk, DN_HEADS * CHUNK), lambda i: (i, 0, 0)),
                 pl.BlockSpec((nchunk, 1, LANES), lambda i: (i, 0, 0))]
    in_specs = [pl.BlockSpec((lp, wqkv), lambda i: (i, 0)),
                pl.BlockSpec((SUBLANES, wqkv), lambda i: (jnp.maximum(i * hb - 1, 0), 0)),
                pl.BlockSpec((lp, LANES), lambda i: (i, 0)),
                pl.BlockSpec(conv_w.shape, lambda i: (0, 0)),
                pl.BlockSpec((1, LANES), lambda i: (0, 0)),
                pl.BlockSpec((1, LANES), lambda i: (0, 0))]
    return pl.pallas_call(
        functools.partial(_dn_prep_kernel, lp=lp, seq=seq, dk=dk),
        grid=(t // lp,), in_specs=in_specs, out_specs=out_specs, out_shape=out_shape,
        scratch_shapes=[pltpu.VMEM((lp + SUBLANES, wqkv), F32)],
        compiler_params=_cparams(("parallel",)), name="dn_prep",
    )(qkv, qkv, ab, conv_w, alog_row, dtb_row)


def _dn_scan_kernel(u_ref, w_ref, qg_ref, qk_ref, kdt_ref, egl_ref, z_ref, gain_ref,
                    o_ref, s_ref, *, ls, dk, nb):
    @pl.when(pl.program_id(1) == 0)
    def _():
        s_ref[...] = jnp.zeros_like(s_ref)

    gain = gain_ref[...]
    heads = range(DN_HEADS)
    cs = [slice(h * dk, (h + 1) * dk) for h in heads]
    hs = [slice(h * CHUNK, (h + 1) * CHUNK) for h in heads]
    items = [(b, h) for b in range(nb) for h in heads]

    def body(c, carry):
        r0 = pl.multiple_of(c * CHUNK, CHUNK)
        rows = pl.ds(r0, CHUNK)
        e_full = [jnp.broadcast_to(egl_ref[b, c], (dk, LANES)) for b in range(nb)]
        ss = [s_ref[b * DN_HEADS + h] for b, h in items]
        sbs = [s.astype(BF16) for s in ss]
        wss = [_dot(w_ref[b, rows, cs[h]], sbs[i]) for i, (b, h) in enumerate(items)]
        oqs = [_dot(qg_ref[b, rows, cs[h]], sbs[i]) for i, (b, h) in enumerate(items)]
        dbs = [(u_ref[b, rows, cs[h]] - wss[i]).astype(BF16) for i, (b, h) in enumerate(items)]
        ods = [_dot(qk_ref[b, rows, hs[h]], dbs[i]) for i, (b, h) in enumerate(items)]
        sds = [_dot(kdt_ref[b, c, :, hs[h]], dbs[i]) for i, (b, h) in enumerate(items)]
        for i, (b, h) in enumerate(items):
            s_ref[b * DN_HEADS + h] = e_full[b][:, h:h + 1] * ss[i] + sds[i]
        for i, (b, h) in enumerate(items):
            o = oqs[i] + ods[i]
            o_ref[b, rows, cs[h]] = (_rms(o, gain) * _silu(z_ref[b, rows, cs[h]])).astype(BF16)
        return carry

    lax.fori_loop(0, ls // CHUNK, body, 0)


def _dn_scan(u, w, qg, qk, kdt, egl, z, gain, batch, seq, ls, nb):
    t, wq = u.shape
    dk = wq // DN_HEADS
    ns = seq // ls
    nchunk = ls // CHUNK
    seq3 = lambda a: a.reshape(batch, seq, a.shape[-1])
    tok = lambda wd: pl.BlockSpec((nb, ls, wd), lambda b, s: (b, s, 0))
    in_specs = [tok(wq), tok(wq), tok(wq), tok(DN_HEADS * CHUNK),
                pl.BlockSpec((nb, nchunk, dk, DN_HEADS * CHUNK), lambda b, s: (b, s, 0, 0)),
                pl.BlockSpec((nb, nchunk, 1, LANES), lambda b, s: (b, s, 0, 0)),
                tok(wq),
                pl.BlockSpec((1, dk), lambda b, s: (0, 0))]
    out = pl.pallas_call(
        functools.partial(_dn_scan_kernel, ls=ls, dk=dk, nb=nb),
        grid=(batch // nb, ns), in_specs=in_specs, out_specs=tok(wq),
        out_shape=jax.ShapeDtypeStruct((batch, seq, wq), BF16),
        scratch_shapes=[pltpu.VMEM((nb * DN_HEADS, dk, dk), F32)],
        compiler_params=_cparams(("parallel", "arbitrary")), name="dn_scan",
    )(seq3(u), seq3(w), seq3(qg), seq3(qk),
      kdt.reshape(batch, seq // CHUNK, dk, DN_HEADS * CHUNK), egl.reshape(batch, seq // CHUNK, 1, LANES),
      seq3(z), gain)
    return out.reshape(t, wq)


def _sb_kernel(q_ref, k_ref, v_ref, o_ref, acc_ref, r_ref, kmax_ref, *, tq, nblk, nh, dh, scale):
    qi = pl.program_id(2)
    rb = tq // nblk
    heads = range(nh)
    hc = [slice(h * dh, (h + 1) * dh) for h in heads]

    @pl.when(qi == 0)
    def _():
        for h in heads:
            kf = k_ref[:, hc[h]].astype(F32)
            kn2 = jnp.sum(kf * kf, axis=-1, keepdims=True)
            kmax_ref[h:h + 1, :] = jnp.broadcast_to(jnp.sqrt(jnp.max(kn2, axis=0, keepdims=True)), (1, LANES))

    zb = []
    for h in heads:
        qf = q_ref[:, hc[h]].astype(F32)
        zb.append(scale * jnp.sqrt(jnp.sum(qf * qf, axis=-1, keepdims=True)) * kmax_ref[h:h + 1, 0:1])
    rr = lax.broadcasted_iota(jnp.int32, (tq, tq), 0)
    cc = lax.broadcasted_iota(jnp.int32, (tq, tq), 1)
    suffix = jnp.where(rr >= cc, 1.0, 0.0).astype(BF16)

    acc_ref[...] = jnp.zeros_like(acc_ref)
    r_ref[...] = jnp.zeros_like(r_ref)

    blocks = range(nblk)
    rs = [slice(b * rb, (b + 1) * rb) for b in blocks]
    kpos = lax.broadcasted_iota(jnp.int32, (rb, tq), 1)
    strict = [kpos < lax.broadcasted_iota(jnp.int32, (rb, tq), 0) + b * rb for b in blocks]

    def tiles(specs):
        tix = range(len(specs))
        items = [(h, t, b) for h in heads for t in tix for b in blocks]
        k0 = [pl.multiple_of(kt * tq, tq) for kt, _, _ in specs]
        zs = {(h, t, b): _dot_nt(q_ref[rs[b], hc[h]], k_ref[pl.ds(k0[t], tq), hc[h]]) * scale
              for h, t, b in items}
        ls = {it: -(jnp.maximum(z, 0.0) + jnp.log(1.0 + jnp.exp(-jnp.abs(z)))) for it, z in zs.items()}
        for h, t, b in items:
            if specs[t][1]:
                ls[h, t, b] = jnp.where(strict[b], ls[h, t, b], 0.0)
        parts = {it: _split2(l) for it, l in ls.items()}
        csums = {it: _dot(hi, suffix) + _dot(lo, suffix) for it, (hi, lo) in parts.items()}
        carry = {(h, b): r_ref[h, rs[b], :] for h in heads for b in blocks}
        atts = {}
        for h, t, b in items:
            att = jnp.exp(zs[h, t, b] + csums[h, t, b] + carry[h, b])
            if specs[t][1]:
                att = jnp.where(strict[b], att, 0.0)
            if specs[t][2] is not None:
                att = jnp.where(specs[t][2], att, 0.0)
            atts[h, t, b] = att.astype(BF16)
            carry[h, b] = carry[h, b] + csums[h, t, b][:, 0:1]
        pvs = {(h, t, b): _dot(atts[h, t, b], v_ref[pl.ds(k0[t], tq), hc[h]]) for h, t, b in items}
        for h in heads:
            for b in blocks:
                acc = acc_ref[rs[b], hc[h]]
                for t in tix:
                    acc = acc + pvs[h, t, b]
                acc_ref[rs[b], hc[h]] = acc
                r_ref[h, rs[b], :] = carry[h, b]

    def live():
        m = jnp.max(r_ref[0] + zb[0])
        for h in range(1, nh):
            m = jnp.maximum(m, jnp.max(r_ref[h] + zb[h]))
        return m

    tiles([(qi, True, None), (jnp.maximum(qi - 1, 0), False, qi > 0)])

    def cond(carry):
        j, m = carry
        return jnp.logical_and(j < qi, m > -SB_UNDERFLOW)

    def body(carry):
        j, _ = carry
        tiles([(qi - 1 - j, False, None)])
        return j + 1, live()

    lax.while_loop(cond, body, (jnp.int32(1), live()))
    o_ref[...] = acc_ref[...].astype(BF16)


def _sb_attention(q, k, v, batch, seq, tq):
    t, wq = q.shape
    dh = wq // SB_HEADS
    nq = seq // tq
    nh = SB_HEADS_PER_STEP
    wh = nh * dh
    return pl.pallas_call(
        functools.partial(_sb_kernel, tq=tq, nblk=SB_ROW_BLOCKS, nh=nh, dh=dh, scale=dh ** -0.5),
        grid=(batch, SB_HEADS // nh, nq),
        in_specs=[pl.BlockSpec((tq, wh), lambda b, h, i: (b * nq + i, h)),
                  pl.BlockSpec((seq, wh), lambda b, h, i: (b, h)),
                  pl.BlockSpec((seq, wh), lambda b, h, i: (b, h))],
        out_specs=pl.BlockSpec((tq, wh), lambda b, h, i: (b * nq + i, h)),
        out_shape=jax.ShapeDtypeStruct((t, wq), BF16),
        scratch_shapes=[pltpu.VMEM((tq, wh), F32), pltpu.VMEM((nh, tq, 1), F32),
                        pltpu.VMEM((SUBLANES, LANES), F32)],
        compiler_params=_cparams(("parallel", "parallel", "arbitrary")), name="sb_attn",
    )(q, k, v)


def _gla_kernel(q_ref, k_ref, gk_ref, v_ref, r_ref, gain_ref, o_ref, s_ref, *, ls, dk, dv):
    @pl.when(pl.program_id(1) == 0)
    def _():
        s_ref[...] = jnp.zeros_like(s_ref)

    gain = gain_ref[...]
    tri = _chunk_tri(ls)
    g1, g2, g3 = _split3(gk_ref[...])
    bc_all = _dot(tri, g1) + _dot(tri, g2) + _dot(tri, g3)
    ii = lax.broadcasted_iota(jnp.int32, (ls, ls), 0)
    jj = lax.broadcasted_iota(jnp.int32, (ls, ls), 1)
    causal = ((ii // CHUNK) == (jj // CHUNK)) & (jj <= ii)
    chunks = range(ls // CHUNK)
    heads = range(GLA_HEADS)
    rs = [slice(c * CHUNK, (c + 1) * CHUNK) for c in chunks]
    ks = [slice(h * dk, (h + 1) * dk) for h in heads]
    vs = [slice(h * dv, (h + 1) * dv) for h in heads]

    def chunk_row(x, r):
        return jnp.concatenate(
            [jnp.broadcast_to(x[c * CHUNK + r:c * CHUNK + r + 1, :], (CHUNK, x.shape[1])) for c in chunks], axis=0)

    bcs = [bc_all[:, ks[h]] for h in heads]
    bmids = [chunk_row(bc, CHUNK // 2) for bc in bcs]
    blasts = [chunk_row(bc, CHUNK - 1) for bc in bcs]
    qs = [q_ref[:, ks[h]] * (dk ** -0.5) for h in heads]
    kk = [k_ref[:, ks[h]] for h in heads]
    vv = [v_ref[:, vs[h]] for h in heads]
    atts = [_dot_nt((qs[h] * jnp.exp(bcs[h] - bmids[h])).astype(BF16),
                    (kk[h] * jnp.exp(bmids[h] - bcs[h])).astype(BF16)) for h in heads]
    atts = [jnp.where(causal, a, 0.0).astype(BF16) for a in atts]
    o_intra = [_dot(atts[h], vv[h]) for h in heads]
    kdts = [(kk[h] * jnp.exp(blasts[h] - bcs[h])).T.astype(BF16) for h in heads]
    bcts = [bc.T for bc in bcs]
    qgs = [(qs[h] * jnp.exp(bcs[h])).astype(BF16) for h in heads]
    upd = [[_dot(kdts[h][:, rs[c]], vv[h][rs[c]]) for c in chunks] for h in heads]
    snaps = []
    for h in heads:
        s = s_ref[h]
        sn = []
        for c in chunks:
            sn.append(s.astype(BF16))
            last = (c + 1) * CHUNK - 1
            s = jnp.exp(bcts[h][:, last:last + 1]) * s + upd[h][c]
        s_ref[h] = s
        snaps.append(sn)
    for h in heads:
        o_inter = jnp.concatenate([_dot(qgs[h][rs[c]], snaps[h][c]) for c in chunks], axis=0)
        o = o_intra[h] + o_inter
        o_ref[:, vs[h]] = (_rms(o, gain) * _silu(r_ref[:, vs[h]])).astype(BF16)


def _gla(q, k, gk, v, r, gain, batch, seq, ls):
    t, wk = q.shape
    wv = v.shape[1]
    dk = wk // GLA_HEADS
    dv = wv // GLA_HEADS
    ns = seq // ls
    tok = lambda wd: pl.BlockSpec((ls, wd), lambda b, s: (b * ns + s, 0))
    return pl.pallas_call(
        functools.partial(_gla_kernel, ls=ls, dk=dk, dv=dv),
        grid=(batch, ns),
        in_specs=[tok(wk), tok(wk), tok(wk), tok(wv), tok(wv),
                  pl.BlockSpec((1, dv), lambda b, s: (0, 0))],
        out_specs=tok(wv),
        out_shape=jax.ShapeDtypeStruct((t, wv), BF16),
        scratch_shapes=[pltpu.VMEM((GLA_HEADS, dk, dv), F32)],
        compiler_params=_cparams(("parallel", "arbitrary")), name="gla",
    )(q, k, gk, v, r, gain)


def _post_kernel(*refs, n_act, tm, seq, nj, tff, dh):
    x_ref = refs[0]
    a_refs = refs[1:1 + n_act]
    wout_refs = refs[1 + n_act:1 + 2 * n_act]
    (gxa_ref, wq_ref, gq_ref, k_ref, v_ref, wo_ref,
     gff_ref, wup_ref, cw_ref, cb_ref, wd_ref, o_ref, act_ref, carry_ref) = refs[1 + 2 * n_act:]

    x = x_ref[...]
    for a_ref, w_ref in zip(a_refs, wout_refs):
        x = x + _dot(a_ref[...], w_ref[...])

    hn = _rms(x, gxa_ref[...]).astype(BF16)
    gq = gq_ref[...]
    heads = range(XA_HEADS)
    hs = [slice(h * dh, (h + 1) * dh) for h in heads]
    qs = [_dot(hn, wq_ref[:, hs[h]]) for h in heads]
    qn = [_rms(q, gq).astype(BF16) for q in qs]
    sc = [_dot_nt(qn[h], k_ref[:, hs[h]]) * (dh ** -0.5) for h in heads]
    es = [jnp.exp(s - jnp.max(s, axis=-1, keepdims=True)) for s in sc]
    pr = [(e / jnp.sum(e, axis=-1, keepdims=True)).astype(BF16) for e in es]
    oh = [_dot(pr[h], v_ref[:, hs[h]]).astype(BF16) for h in heads]
    x = x + _dot(jnp.concatenate(oh, axis=1), wo_ref[...])

    hn = _rms(x, gff_ref[...]).astype(BF16)
    first = (pl.program_id(0) * tm) % seq == 0
    kw = cw_ref.shape[0]

    def branch(idx):
        cols = slice(idx * tff, (idx + 1) * tff)
        u = _dot(hn, wup_ref[:, cols])
        halo = jnp.where(first, 0.0, carry_ref[idx])
        carry_ref[idx] = u[tm - SUBLANES:tm, :]
        ext = jnp.concatenate([halo, u], axis=0)
        cw = cw_ref[:, cols]
        y = cb_ref[:, cols] + cw[kw - 1:kw, :] * u
        for s in range(1, kw):
            y = y + cw[kw - 1 - s:kw - s, :] * ext[SUBLANES - s:SUBLANES - s + tm, :]
        return y

    for j in range(nj):
        gate = branch(j)
        val = branch(nj + j)
        act_ref[:, j * tff:(j + 1) * tff] = (_silu(gate) * val).astype(BF16)
    o_ref[...] = x + _dot(act_ref[...], wd_ref[...])


def _post(x, acts, wouts, g_xa, wq, gq, kmem, vmem, wo, g_ffn, w_up, conv_w, conv_b, w_down, seq, tm, tff):
    t, d = x.shape
    dh = gq.shape[0]
    nm = kmem.shape[0] // (t // seq)
    per = seq // tm
    dff = w_down.shape[0]
    nj = dff // tff
    cb = conv_b.reshape(1, -1)
    resident = lambda shape: pl.BlockSpec(shape, lambda i: (0,) * len(shape), pipeline_mode=pl.Buffered(1))
    tok = lambda wd: pl.BlockSpec((tm, wd), lambda i: (i, 0))
    in_specs = [tok(d)] + [tok(a.shape[1]) for a in acts] + [resident(w.shape) for w in wouts]
    in_specs += [resident((1, d)), resident((d, d)), resident((1, dh)),
                 pl.BlockSpec((nm, d), lambda i: (i // per, 0)),
                 pl.BlockSpec((nm, d), lambda i: (i // per, 0)),
                 resident((d, d)),
                 resident((1, d)), resident(w_up.shape), resident(conv_w.shape), resident(cb.shape),
                 resident(w_down.shape)]
    return pl.pallas_call(
        functools.partial(_post_kernel, n_act=len(acts), tm=tm, seq=seq, nj=nj, tff=tff, dh=dh),
        grid=(t // tm,), in_specs=in_specs, out_specs=tok(d),
        out_shape=jax.ShapeDtypeStruct((t, d), F32),
        scratch_shapes=[pltpu.VMEM((tm, dff), BF16),
                        pltpu.VMEM((2 * nj, SUBLANES, tff), F32)],
        compiler_params=_cparams(("arbitrary",)), name="post",
    )(x, *acts, *wouts, g_xa.reshape(1, d), wq, gq.reshape(1, dh), kmem, vmem, wo,
      g_ffn.reshape(1, d), w_up, conv_w, cb, w_down)


def _pad_cols(w, n):
    return jnp.pad(w, ((0, 0), (0, n - w.shape[1])))


def _mixer_ab(x, g, w_in, conv_w, a_log, dt_bias, dn_g, sb_gq, sb_gk, w_out, batch, seq):
    dk = dn_g.shape[0]
    wq = DN_HEADS * dk
    dh = sb_gq.shape[0]
    wb = SB_HEADS * dh
    c_ab = 4 * wq
    w = jnp.concatenate([w_in[:, :c_ab], _pad_cols(w_in[:, c_ab:c_ab + 2 * DN_HEADS], LANES),
                         w_in[:, c_ab + 2 * DN_HEADS:]], axis=1).astype(BF16)
    o_qb = c_ab + LANES
    plan = [(0, 3 * wq, "raw", 0), (3 * wq, wq, "raw", 0), (c_ab, LANES, "raw", 0),
            (o_qb, wb, "headnorm", dh), (o_qb + wb, wb, "headnorm", dh), (o_qb + 2 * wb, wb, "raw", 0)]
    outs = [(3 * wq, F32), (wq, F32), (LANES, F32), (wb, BF16), (wb, BF16), (wb, BF16)]
    qkv_a, z_a, ab, q_b, k_b, v_b = _proj(
        x, g, w, [sb_gq.reshape(1, dh), sb_gk.reshape(1, dh)], plan, outs, tm=512)

    alog_row = _pad_cols(a_log.reshape(1, -1), LANES)
    dtb_row = _pad_cols(dt_bias.reshape(1, -1), LANES)
    u, wmat, qg, qk, kdt, egl = _dn_prep(qkv_a, ab, conv_w, alog_row, dtb_row, seq, lp=256)
    nb = 4 if batch % 4 == 0 else (2 if batch % 2 == 0 else 1)
    o_a = _dn_scan(u, wmat, qg, qk, kdt, egl, z_a, dn_g.reshape(1, dk), batch, seq, ls=256, nb=nb)
    o_b = _sb_attention(q_b, k_b, v_b, batch, seq, tq=256)
    wo = w_out.astype(BF16)
    return [o_a, o_b], [wo[:wq], wo[wq:]]


def _mixer_gla(x, g, w_in, w_gk, b_gk, norm_g, w_out, batch, seq):
    rank, wk = w_gk.shape
    dv = norm_g.shape[0]
    wv = GLA_HEADS * dv
    w = _pad_cols(w_in, 2 * wk + 2 * wv + LANES).astype(BF16)
    wgk = jnp.pad(w_gk, ((0, LANES - rank), (0, 0))).astype(BF16)
    plan = [(0, wk, "raw", 0), (wk, wk, "raw", 0), (2 * wk, wv, "raw", 0),
            (2 * wk + wv, wv, "raw", 0), (2 * wk + 2 * wv, LANES, "gk", 0)]
    outs = [(wk, F32), (wk, F32), (wv, BF16), (wv, F32), (wk, F32)]
    q, k, v, r, gk = _proj(x, g, w, [wgk, b_gk.reshape(1, wk)], plan, outs, tm=512)
    o = _gla(q, k, gk, v, r, norm_g.reshape(1, dv), batch, seq, ls=256)
    return [o], [w_out.astype(BF16)]


def _mem_kv(mem, g, w_kv, g_k):
    d = mem.shape[1]
    dh = g_k.shape[0]
    plan = [(0, d, "headnorm", dh), (d, d, "raw", 0)]
    outs = [(d, BF16), (d, BF16)]
    return _proj(mem, g, w_kv.astype(BF16), [g_k.reshape(1, dh)], plan, outs, tm=256)


def kernel(x, mem, norm_mix, norm_xa, norm_mem, norm_ffn, xa_w_q, xa_w_kv, xa_w_o, xa_g_q, xa_g_k, ffn_w_up, ffn_conv_w, ffn_conv_b, ffn_w_down, ab_w_in, ab_conv_w, dn_a_log, dn_dt_bias, dn_norm_g, sb_g_q, sb_g_k, ab_w_out, gla_w_in, gla_w_gk, gla_b_gk, gla_norm_g, gla_w_out):
    batch, seq, d = x.shape
    depth = norm_mix.shape[0]
    xf = x.reshape(batch * seq, d)
    memf = mem.reshape(batch * mem.shape[1], d)
    for layer in range(depth):
        i = layer // 2
        if layer % 2 == 0:
            acts, wouts = _mixer_ab(xf, norm_mix[layer], ab_w_in[i], ab_conv_w[i], dn_a_log[i], dn_dt_bias[i],
                                    dn_norm_g[i], sb_g_q[i], sb_g_k[i], ab_w_out[i], batch, seq)
        else:
            acts, wouts = _mixer_gla(xf, norm_mix[layer], gla_w_in[i], gla_w_gk[i], gla_b_gk[i],
                                     gla_norm_g[i], gla_w_out[i], batch, seq)
        kmem, vmem = _mem_kv(memf, norm_mem[layer], xa_w_kv[layer], xa_g_k[layer])
        xf = _post(xf, acts, wouts, norm_xa[layer], xa_w_q[layer].astype(BF16), xa_g_q[layer], kmem, vmem,
                   xa_w_o[layer].astype(BF16), norm_ffn[layer], ffn_w_up[layer].astype(BF16),
                   ffn_conv_w[layer], ffn_conv_b[layer], ffn_w_down[layer].astype(BF16), seq, tm=512, tff=256)
    return xf.reshape(batch, seq, d)
```

```python
import functools

import jax
import jax.numpy as jnp
from jax import lax
from jax.experimental import pallas as pl
from jax.experimental.pallas import tpu as pltpu

F32 = jnp.float32
BF16 = jnp.bfloat16
EPS = 1e-6

DN_HEADS = 4
SB_HEADS = 4
GLA_HEADS = 4
XA_HEADS = 4
GLA_TAU = 16.0
CHUNK = 64
LANES = 128
SUBLANES = 8
VMEM_LIMIT = 56 * 1024 * 1024
SB_UNDERFLOW = 110.0
SB_ROW_BLOCKS = 2
SB_HEADS_PER_STEP = 2
TM_PROJ = 512
TM_MEM = 256
T_DN_PREP = 256
T_DN_SCAN = 1024
T_GLA = 256
T_SB = 256
T_FF = 256


def _cparams(sem):
    return pltpu.CompilerParams(dimension_semantics=sem, vmem_limit_bytes=VMEM_LIMIT)


def _rms(x, g):
    ms = jnp.mean(x * x, axis=-1, keepdims=True)
    return x * lax.rsqrt(ms + EPS) * g


def _l2n(x):
    return x * lax.rsqrt(jnp.sum(x * x, axis=-1, keepdims=True) + EPS)


def _dot(a, b):
    return jnp.dot(a, b, preferred_element_type=F32)


def _dot_nt(a, b):
    return lax.dot_general(a, b, (((1,), (1,)), ((), ())), preferred_element_type=F32)


def _split2(x):
    hi = x.astype(BF16)
    lo = (x - hi.astype(F32)).astype(BF16)
    return hi, lo


def _split3(x):
    hi = x.astype(BF16)
    r = x - hi.astype(F32)
    mid = r.astype(BF16)
    lo = (r - mid.astype(F32)).astype(BF16)
    return hi, mid, lo


def _sigmoid(x):
    return 1.0 / (1.0 + jnp.exp(-x))


def _silu(x):
    return x * _sigmoid(x)


def _softplus(x):
    return jnp.maximum(x, 0.0) + jnp.log1p(jnp.exp(-jnp.abs(x)))


def _chunk_tri(n):
    r = lax.broadcasted_iota(jnp.int32, (n, n), 0)
    c = lax.broadcasted_iota(jnp.int32, (n, n), 1)
    same = (r // CHUNK) == (c // CHUNK)
    return jnp.where(same & (c <= r), 1.0, 0.0).astype(BF16)


def _proj_kernel(*refs, plan, n_aux):
    x_ref, g_ref, w_ref = refs[:3]
    aux = refs[3:3 + n_aux]
    outs = refs[3 + n_aux:]
    hn = _rms(x_ref[...], g_ref[...]).astype(BF16)
    ai = 0
    for (c0, width, mode, hd), o_ref in zip(plan, outs):
        if mode == "headnorm":
            gain = aux[ai][...]
            ai += 1
        elif mode == "gk":
            wgk_ref, bgk_ref = aux[ai], aux[ai + 1]
            ai += 2
        step = min(width, 512)
        for s0 in range(0, width, step):
            p = _dot(hn, w_ref[:, c0 + s0:c0 + s0 + step])
            if mode == "raw":
                o_ref[:, s0:s0 + step] = p.astype(o_ref.dtype)
            elif mode == "headnorm":
                for h0 in range(0, step, hd):
                    o_ref[:, s0 + h0:s0 + h0 + hd] = _rms(p[:, h0:h0 + hd], gain).astype(o_ref.dtype)
            else:
                y = _dot(p.astype(BF16), wgk_ref[...]) + bgk_ref[...]
                o_ref[...] = (jnp.minimum(y, 0.0) - jnp.log1p(jnp.exp(-jnp.abs(y)))) / GLA_TAU


def _proj(x, g, w, aux, plan, out_defs, tm):
    t, d = x.shape
    n = w.shape[1]
    in_specs = [pl.BlockSpec((tm, d), lambda i: (i, 0)),
                pl.BlockSpec((1, d), lambda i: (0, 0)),
                pl.BlockSpec((d, n), lambda i: (0, 0))]
    in_specs += [pl.BlockSpec(a.shape, lambda i: (0, 0)) for a in aux]
    out_shape = [jax.ShapeDtypeStruct((t, wd), dt) for wd, dt in out_defs]
    out_specs = [pl.BlockSpec((tm, wd), lambda i: (i, 0)) for wd, _ in out_defs]
    return pl.pallas_call(
        functools.partial(_proj_kernel, plan=tuple(plan), n_aux=len(aux)),
        grid=(t // tm,), in_specs=in_specs, out_specs=out_specs, out_shape=out_shape,
        compiler_params=_cparams(("parallel",)), name="proj",
    )(x, g.reshape(1, d), w, *aux)


def _dn_prep_kernel(qkv_ref, halo_ref, ab_ref, cw_ref, alog_ref, dtb_ref,
                    u_ref, w_ref, qg_ref, qk_ref, kdt_ref, egl_ref, buf_ref, *, lp, seq, dk):
    nh = DN_HEADS
    wq = nh * dk
    first = (pl.program_id(0) * lp) % seq == 0
    buf_ref[0:SUBLANES, :] = jnp.where(first, 0.0, halo_ref[...])
    buf_ref[SUBLANES:SUBLANES + lp, :] = qkv_ref[...]

    ab = ab_ref[...]
    g = -jnp.exp(alog_ref[...]) * _softplus(ab + dtb_ref[...])
    beta = _sigmoid(ab)
    tri = _chunk_tri(lp)
    g1, g2, g3 = _split3(g)
    gc = _dot(tri, g1) + _dot(tri, g2) + _dot(tri, g3)
    gct = gc.T
    nchunk = lp // CHUNK
    gl_rows = jnp.concatenate(
        [jnp.broadcast_to(gc[(c + 1) * CHUNK - 1:(c + 1) * CHUNK, :], (CHUNK, LANES))
         for c in range(nchunk)], axis=0)
    for c in range(nchunk):
        egl_ref[c] = jnp.exp(gc[(c + 1) * CHUNK - 1:(c + 1) * CHUNK, :])

    cw = cw_ref[...]
    kw = cw.shape[0]

    def convsilu(c0):
        cur = buf_ref[SUBLANES:SUBLANES + lp, c0:c0 + dk]
        halo = buf_ref[0:SUBLANES, c0:c0 + dk]
        y = cw[kw - 1:kw, c0:c0 + dk] * cur
        row = lax.broadcasted_iota(jnp.int32, (SUBLANES, dk), 0)
        for s in range(1, kw):
            shifted = pltpu.roll(cur, s, axis=0)
            head = jnp.where(row < s, pltpu.roll(halo, s, axis=0), shifted[:SUBLANES])
            shifted = jnp.concatenate([head, shifted[SUBLANES:]], axis=0)
            y = y + cw[kw - 1 - s:kw - s, c0:c0 + dk] * shifted
        return _silu(y)

    pb = 2 * CHUNK
    pairs = range(lp // pb)
    ii = lax.broadcasted_iota(jnp.int32, (pb, pb), 0)
    jj = lax.broadcasted_iota(jnp.int32, (pb, pb), 1)
    same = (ii // CHUNK) == (jj // CHUNK)
    causal = same & (jj <= ii)
    strict = same & (jj < ii)
    eye = jnp.where(ii == jj, 1.0, 0.0)

    a_mats, vbs, kbgs = [], [], []
    for h in range(nh):
        cs = slice(h * dk, (h + 1) * dk)
        qn = _l2n(convsilu(h * dk)) * (dk ** -0.5)
        kn = _l2n(convsilu(wq + h * dk))
        v = convsilu(2 * wq + h * dk)
        gcol = gc[:, h:h + 1]
        bcol = beta[:, nh + h:nh + h + 1]
        egc = jnp.exp(gcol)
        kb = kn * bcol
        vbs.append(v * bcol)
        kbgs.append((kb * egc).astype(BF16))
        qg_ref[:, cs] = (qn * egc).astype(BF16)
        kdec = kn * jnp.exp(gl_rows[:, h:h + 1] - gcol)
        kdt = kdec.T.astype(BF16)
        for c in range(nchunk):
            kdt_ref[c, :, h * CHUNK:(h + 1) * CHUNK] = kdt[:, c * CHUNK:(c + 1) * CHUNK]
        knb = kn.astype(BF16)
        kbb = kb.astype(BF16)
        qnb = qn.astype(BF16)
        for p in pairs:
            pr = slice(p * pb, (p + 1) * pb)
            diff = gcol[pr] - gct[h:h + 1, pr]
            decay = jnp.where(causal, jnp.exp(jnp.where(causal, diff, 0.0)), 0.0)
            a_mats.append(jnp.where(strict, _dot_nt(kbb[pr], knb[pr]) * decay, 0.0))
            qk = (_dot_nt(qnb[pr], knb[pr]) * decay).astype(BF16)
            for half in range(2):
                sub = slice(half * CHUNK, (half + 1) * CHUNK)
                r0 = p * pb + half * CHUNK
                qk_ref[r0:r0 + CHUNK, h * CHUNK:(h + 1) * CHUNK] = qk[sub, sub]

    x_invs = [eye - a for a in a_mats]
    ps = a_mats
    for _ in range(5):
        pbs = [p.astype(BF16) for p in ps]
        ps = [_dot(b, b) for b in pbs]
        x_invs = [x + _dot(x.astype(BF16), p.astype(BF16)) for x, p in zip(x_invs, ps)]
    for h in range(nh):
        cs = slice(h * dk, (h + 1) * dk)
        for p in pairs:
            pr = slice(p * pb, (p + 1) * pb)
            x1, x2 = _split2(x_invs[h * len(pairs) + p])
            v1, v2 = _split2(vbs[h][pr])
            u_ref[pr, cs] = _dot(x1, v1) + _dot(x1, v2) + _dot(x2, v1)
            w_ref[pr, cs] = _dot(x1, kbgs[h][pr]).astype(BF16)


def _dn_prep(qkv, ab, conv_w, alog_row, dtb_row, seq, lp):
    t, wqkv = qkv.shape
    dk = wqkv // (3 * DN_HEADS)
    wq = DN_HEADS * dk
    hb = lp // SUBLANES
    nchunk = lp // CHUNK
    out_shape = [jax.ShapeDtypeStruct((t, wq), F32),
                 jax.ShapeDtypeStruct((t, wq), BF16),
                 jax.ShapeDtypeStruct((t, wq), BF16),
                 jax.ShapeDtypeStruct((t, DN_HEADS * CHUNK), BF16),
                 jax.ShapeDtypeStruct((t // CHUNK, dk, DN_HEADS * CHUNK), BF16),
                 jax.ShapeDtypeStruct((t // CHUNK, 1, LANES), F32)]
    out_specs = [pl.BlockSpec((lp, wq), lambda i: (i, 0)),
                 pl.BlockSpec((lp, wq), lambda i: (i, 0)),
                 pl.BlockSpec((lp, wq), lambda i: (i, 0)),
                 pl.BlockSpec((lp, DN_HEADS * CHUNK), lambda i: (i, 0)),
                 pl.BlockSpec((nchunk, dk, DN_HEADS * CHUNK), lambda i: (i, 0, 0)),
                 pl.BlockSpec((nchunk, 1, LANES), lambda i: (i, 0, 0))]
    in_specs = [pl.BlockSpec((lp, wqkv), lambda i: (i, 0)),
                pl.BlockSpec((SUBLANES, wqkv), lambda i: (jnp.maximum(i * hb - 1, 0), 0)),
                pl.BlockSpec((lp, LANES), lambda i: (i, 0)),
                pl.BlockSpec(conv_w.shape, lambda i: (0, 0)),
                pl.BlockSpec((1, LANES), lambda i: (0, 0)),
                pl.BlockSpec((1, LANES), lambda i: (0, 0))]
    return pl.pallas_call(
        functools.partial(_dn_prep_kernel, lp=lp, seq=seq, dk=dk),
        grid=(t // lp,), in_specs=in_specs, out_specs=out_specs, out_shape=out_shape,
        scratch_shapes=[pltpu.VMEM((lp + SUBLANES, wqkv), F32)],
        compiler_params=_cparams(("parallel",)), name="dn_prep",
    )(qkv, qkv, ab, conv_w, alog_row, dtb_row)


def _dn_scan_kernel(u_ref, w_ref, qg_ref, qk_ref, kdt_ref, egl_ref, z_ref, gain_ref,
                    o_ref, s_ref, *, ls, dk, nb):
    @pl.when(pl.program_id(1) == 0)
    def _():
        s_ref[...] = jnp.zeros_like(s_ref)

    gain = gain_ref[...]
    heads = range(DN_HEADS)
    cs = [slice(h * dk, (h + 1) * dk) for h in heads]
    hs = [slice(h * CHUNK, (h + 1) * CHUNK) for h in heads]
    items = [(b, h) for b in range(nb) for h in heads]

    def body(c, carry):
        r0 = pl.multiple_of(c * CHUNK, CHUNK)
        rows = pl.ds(r0, CHUNK)
        e_full = [jnp.broadcast_to(egl_ref[b, c], (dk, LANES)) for b in range(nb)]
        ss = [s_ref[b * DN_HEADS + h] for b, h in items]
        sbs = [s.astype(BF16) for s in ss]
        wss = [_dot(w_ref[b, rows, cs[h]], sbs[i]) for i, (b, h) in enumerate(items)]
        oqs = [_dot(qg_ref[b, rows, cs[h]], sbs[i]) for i, (b, h) in enumerate(items)]
        dbs = [(u_ref[b, rows, cs[h]] - wss[i]).astype(BF16) for i, (b, h) in enumerate(items)]
        ods = [_dot(qk_ref[b, rows, hs[h]], dbs[i]) for i, (b, h) in enumerate(items)]
        sds = [_dot(kdt_ref[b, c, :, hs[h]], dbs[i]) for i, (b, h) in enumerate(items)]
        for i, (b, h) in enumerate(items):
            s_ref[b * DN_HEADS + h] = e_full[b][:, h:h + 1] * ss[i] + sds[i]
        for i, (b, h) in enumerate(items):
            o = oqs[i] + ods[i]
            o_ref[b, rows, cs[h]] = (_rms(o, gain) * _silu(z_ref[b, rows, cs[h]])).astype(BF16)
        return carry

    lax.fori_loop(0, ls // CHUNK, body, 0)


def _dn_scan(u, w, qg, qk, kdt, egl, z, gain, batch, seq, ls, nb):
    t, wq = u.shape
    dk = wq // DN_HEADS
    ns = seq // ls
    nchunk = ls // CHUNK
    seq3 = lambda a: a.reshape(batch, seq, a.shape[-1])
    tok = lambda wd: pl.BlockSpec((nb, ls, wd), lambda b, s: (b, s, 0))
    in_specs = [tok(wq), tok(wq), tok(wq), tok(DN_HEADS * CHUNK),
                pl.BlockSpec((nb, nchunk, dk, DN_HEADS * CHUNK), lambda b, s: (b, s, 0, 0)),
                pl.BlockSpec((nb, nchunk, 1, LANES), lambda b, s: (b, s, 0, 0)),
                tok(wq),
                pl.BlockSpec((1, dk), lambda b, s: (0, 0))]
    out = pl.pallas_call(
        functools.partial(_dn_scan_kernel, ls=ls, dk=dk, nb=nb),
        grid=(batch // nb, ns), in_specs=in_specs, out_specs=tok(wq),
        out_shape=jax.ShapeDtypeStruct((batch, seq, wq), BF16),
        scratch_shapes=[pltpu.VMEM((nb * DN_HEADS, dk, dk), F32)],
        compiler_params=_cparams(("parallel", "arbitrary")), name="dn_scan",
    )(seq3(u), seq3(w), seq3(qg), seq3(qk),
      kdt.reshape(batch, seq // CHUNK, dk, DN_HEADS * CHUNK), egl.reshape(batch, seq // CHUNK, 1, LANES),
      seq3(z), gain)
    return out.reshape(t, wq)


def _sb_kernel(q_ref, k_ref, v_ref, o_ref, acc_ref, r_ref, kmax_ref, *, tq, nblk, nh, dh, scale):
    qi = pl.program_id(2)
    rb = tq // nblk
    heads = range(nh)
    hc = [slice(h * dh, (h + 1) * dh) for h in heads]

    @pl.when(qi == 0)
    def _():
        for h in heads:
            kf = k_ref[:, hc[h]].astype(F32)
            kn2 = jnp.sum(kf * kf, axis=-1, keepdims=True)
            kmax_ref[h:h + 1, :] = jnp.broadcast_to(jnp.sqrt(jnp.max(kn2, axis=0, keepdims=True)), (1, LANES))

    zb = []
    for h in heads:
        qf = q_ref[:, hc[h]].astype(F32)
        zb.append(scale * jnp.sqrt(jnp.sum(qf * qf, axis=-1, keepdims=True)) * kmax_ref[h:h + 1, 0:1])
    rr = lax.broadcasted_iota(jnp.int32, (tq, tq), 0)
    cc = lax.broadcasted_iota(jnp.int32, (tq, tq), 1)
    suffix = jnp.where(rr >= cc, 1.0, 0.0).astype(BF16)

    acc_ref[...] = jnp.zeros_like(acc_ref)
    r_ref[...] = jnp.zeros_like(r_ref)

    blocks = range(nblk)
    rs = [slice(b * rb, (b + 1) * rb) for b in blocks]
    kpos = lax.broadcasted_iota(jnp.int32, (rb, tq), 1)
    strict = [kpos < lax.broadcasted_iota(jnp.int32, (rb, tq), 0) + b * rb for b in blocks]

    def tiles(specs):
        tix = range(len(specs))
        items = [(h, t, b) for h in heads for t in tix for b in blocks]
        k0 = [pl.multiple_of(kt * tq, tq) for kt, _, _ in specs]
        zs = {(h, t, b): _dot_nt(q_ref[rs[b], hc[h]], k_ref[pl.ds(k0[t], tq), hc[h]]) * scale
              for h, t, b in items}
        ls = {it: -(jnp.maximum(z, 0.0) + jnp.log(1.0 + jnp.exp(-jnp.abs(z)))) for it, z in zs.items()}
        for h, t, b in items:
            if specs[t][1]:
                ls[h, t, b] = jnp.where(strict[b], ls[h, t, b], 0.0)
        parts = {it: _split2(l) for it, l in ls.items()}
        csums = {it: _dot(hi, suffix) + _dot(lo, suffix) for it, (hi, lo) in parts.items()}
        carry = {(h, b): r_ref[h, rs[b], :] for h in heads for b in blocks}
        atts = {}
        for h, t, b in items:
            att = jnp.exp(zs[h, t, b] + csums[h, t, b] + carry[h, b])
            if specs[t][1]:
                att = jnp.where(strict[b], att, 0.0)
            if specs[t][2] is not None:
                att = jnp.where(specs[t][2], att, 0.0)
            atts[h, t, b] = att.astype(BF16)
            carry[h, b] = carry[h, b] + csums[h, t, b][:, 0:1]
        pvs = {(h, t, b): _dot(atts[h, t, b], v_ref[pl.ds(k0[t], tq), hc[h]]) for h, t, b in items}
        for h in heads:
            for b in blocks:
                acc = acc_ref[rs[b], hc[h]]
                for t in tix:
                    acc = acc + pvs[h, t, b]
                acc_ref[rs[b], hc[h]] = acc
                r_ref[h, rs[b], :] = carry[h, b]

    def live():
        m = jnp.max(r_ref[0] + zb[0])
        for h in range(1, nh):
            m = jnp.maximum(m, jnp.max(r_ref[h] + zb[h]))
        return m

    tiles([(qi, True, None), (jnp.maximum(qi - 1, 0), False, qi > 0)])

    def cond(carry):
        j, m = carry
        return jnp.logical_and(j < qi, m > -SB_UNDERFLOW)

    def body(carry):
        j, _ = carry
        tiles([(qi - 1 - j, False, None)])
        return j + 1, live()

    lax.while_loop(cond, body, (jnp.int32(1), live()))
    o_ref[...] = acc_ref[...].astype(BF16)


def _sb_attention(q, k, v, batch, seq, tq):
    t, wq = q.shape
    dh = wq // SB_HEADS
    nq = seq // tq
    nh = SB_HEADS_PER_STEP
    wh = nh * dh
    return pl.pallas_call(
        functools.partial(_sb_kernel, tq=tq, nblk=SB_ROW_BLOCKS, nh=nh, dh=dh, scale=dh ** -0.5),
        grid=(batch, SB_HEADS // nh, nq),
        in_specs=[pl.BlockSpec((tq, wh), lambda b, h, i: (b * nq + i, h)),
                  pl.BlockSpec((seq, wh), lambda b, h, i: (b, h)),
                  pl.BlockSpec((seq, wh), lambda b, h, i: (b, h))],
        out_specs=pl.BlockSpec((tq, wh), lambda b, h, i: (b * nq + i, h)),
        out_shape=jax.ShapeDtypeStruct((t, wq), BF16),
        scratch_shapes=[pltpu.VMEM((tq, wh), F32), pltpu.VMEM((nh, tq, 1), F32),
                        pltpu.VMEM((SUBLANES, LANES), F32)],
        compiler_params=_cparams(("parallel", "parallel", "arbitrary")), name="sb_attn",
    )(q, k, v)


def _gla_kernel(q_ref, k_ref, gk_ref, v_ref, r_ref, gain_ref, o_ref, s_ref, *, ls, dk, dv):
    @pl.when(pl.program_id(1) == 0)
    def _():
        s_ref[...] = jnp.zeros_like(s_ref)

    gain = gain_ref[...]
    tri = _chunk_tri(ls)
    g1, g2, g3 = _split3(gk_ref[...])
    bc_all = _dot(tri, g1) + _dot(tri, g2) + _dot(tri, g3)
    ii = lax.broadcasted_iota(jnp.int32, (ls, ls), 0)
    jj = lax.broadcasted_iota(jnp.int32, (ls, ls), 1)
    causal = ((ii // CHUNK) == (jj // CHUNK)) & (jj <= ii)
    chunks = range(ls // CHUNK)
    heads = range(GLA_HEADS)
    rs = [slice(c * CHUNK, (c + 1) * CHUNK) for c in chunks]
    ks = [slice(h * dk, (h + 1) * dk) for h in heads]
    vs = [slice(h * dv, (h + 1) * dv) for h in heads]

    def chunk_row(x, r):
        return jnp.concatenate(
            [jnp.broadcast_to(x[c * CHUNK + r:c * CHUNK + r + 1, :], (CHUNK, x.shape[1])) for c in chunks], axis=0)

    bcs = [bc_all[:, ks[h]] for h in heads]
    bmids = [chunk_row(bc, CHUNK // 2) for bc in bcs]
    blasts = [chunk_row(bc, CHUNK - 1) for bc in bcs]
    qs = [q_ref[:, ks[h]] * (dk ** -0.5) for h in heads]
    kk = [k_ref[:, ks[h]] for h in heads]
    vv = [v_ref[:, vs[h]] for h in heads]
    atts = [_dot_nt((qs[h] * jnp.exp(bcs[h] - bmids[h])).astype(BF16),
                    (kk[h] * jnp.exp(bmids[h] - bcs[h])).astype(BF16)) for h in heads]
    atts = [jnp.where(causal, a, 0.0).astype(BF16) for a in atts]
    o_intra = [_dot(atts[h], vv[h]) for h in heads]
    kdts = [(kk[h] * jnp.exp(blasts[h] - bcs[h])).T.astype(BF16) for h in heads]
    bcts = [bc.T for bc in bcs]
    qgs = [(qs[h] * jnp.exp(bcs[h])).astype(BF16) for h in heads]
    upd = [[_dot(kdts[h][:, rs[c]], vv[h][rs[c]]) for c in chunks] for h in heads]
    snaps = []
    for h in heads:
        s = s_ref[h]
        sn = []
        for c in chunks:
            sn.append(s.astype(BF16))
            last = (c + 1) * CHUNK - 1
            s = jnp.exp(bcts[h][:, last:last + 1]) * s + upd[h][c]
        s_ref[h] = s
        snaps.append(sn)
    for h in heads:
        o_inter = jnp.concatenate([_dot(qgs[h][rs[c]], snaps[h][c]) for c in chunks], axis=0)
        o = o_intra[h] + o_inter
        o_ref[:, vs[h]] = (_rms(o, gain) * _silu(r_ref[:, vs[h]])).astype(BF16)


def _gla(q, k, gk, v, r, gain, batch, seq, ls):
    t, wk = q.shape
    wv = v.shape[1]
    dk = wk // GLA_HEADS
    dv = wv // GLA_HEADS
    ns = seq // ls
    tok = lambda wd: pl.BlockSpec((ls, wd), lambda b, s: (b * ns + s, 0))
    return pl.pallas_call(
        functools.partial(_gla_kernel, ls=ls, dk=dk, dv=dv),
        grid=(batch, ns),
        in_specs=[tok(wk), tok(wk), tok(wk), tok(wv), tok(wv),
                  pl.BlockSpec((1, dv), lambda b, s: (0, 0))],
        out_specs=tok(wv),
        out_shape=jax.ShapeDtypeStruct((t, wv), BF16),
        scratch_shapes=[pltpu.VMEM((GLA_HEADS, dk, dv), F32)],
        compiler_params=_cparams(("parallel", "arbitrary")), name="gla",
    )(q, k, gk, v, r, gain)


def _post_kernel(*refs, n_act, tm, seq, nj, tff, dh):
    x_ref = refs[0]
    a_refs = refs[1:1 + n_act]
    wout_refs = refs[1 + n_act:1 + 2 * n_act]
    (gxa_ref, wq_ref, gq_ref, k_ref, v_ref, wo_ref,
     gff_ref, wup_ref, cw_ref, cb_ref, wd_ref, o_ref, act_ref, carry_ref) = refs[1 + 2 * n_act:]

    x = x_ref[...]
    for a_ref, w_ref in zip(a_refs, wout_refs):
        x = x + _dot(a_ref[...], w_ref[...])

    hn = _rms(x, gxa_ref[...]).astype(BF16)
    gq = gq_ref[...]
    heads = range(XA_HEADS)
    hs = [slice(h * dh, (h + 1) * dh) for h in heads]
    qs = [_dot(hn, wq_ref[:, hs[h]]) for h in heads]
    qn = [_rms(q, gq).astype(BF16) for q in qs]
    sc = [_dot_nt(qn[h], k_ref[:, hs[h]]) * (dh ** -0.5) for h in heads]
    es = [jnp.exp(s - jnp.max(s, axis=-1, keepdims=True)) for s in sc]
    pr = [(e / jnp.sum(e, axis=-1, keepdims=True)).astype(BF16) for e in es]
    oh = [_dot(pr[h], v_ref[:, hs[h]]).astype(BF16) for h in heads]
    x = x + _dot(jnp.concatenate(oh, axis=1), wo_ref[...])

    hn = _rms(x, gff_ref[...]).astype(BF16)
    first = (pl.program_id(0) * tm) % seq == 0
    kw = cw_ref.shape[0]

    def branch(idx):
        cols = slice(idx * tff, (idx + 1) * tff)
        u = _dot(hn, wup_ref[:, cols])
        halo = jnp.where(first, 0.0, carry_ref[idx])
        carry_ref[idx] = u[tm - SUBLANES:tm, :]
        ext = jnp.concatenate([halo, u], axis=0)
        cw = cw_ref[:, cols]
        y = cb_ref[:, cols] + cw[kw - 1:kw, :] * u
        for s in range(1, kw):
            y = y + cw[kw - 1 - s:kw - s, :] * ext[SUBLANES - s:SUBLANES - s + tm, :]
        return y

    for j in range(nj):
        gate = branch(j)
        val = branch(nj + j)
        act_ref[:, j * tff:(j + 1) * tff] = (_silu(gate) * val).astype(BF16)
    o_ref[...] = x + _dot(act_ref[...], wd_ref[...])


def _post(x, acts, wouts, g_xa, wq, gq, kmem, vmem, wo, g_ffn, w_up, conv_w, conv_b, w_down, seq, tm, tff):
    t, d = x.shape
    dh = gq.shape[0]
    nm = kmem.shape[0] // (t // seq)
    per = seq // tm
    dff = w_down.shape[0]
    nj = dff // tff
    cb = conv_b.reshape(1, -1)
    resident = lambda shape: pl.BlockSpec(shape, lambda i: (0,) * len(shape), pipeline_mode=pl.Buffered(1))
    tok = lambda wd: pl.BlockSpec((tm, wd), lambda i: (i, 0))
    in_specs = [tok(d)] + [tok(a.shape[1]) for a in acts] + [resident(w.shape) for w in wouts]
    in_specs += [resident((1, d)), resident((d, d)), resident((1, dh)),
                 pl.BlockSpec((nm, d), lambda i: (i // per, 0)),
                 pl.BlockSpec((nm, d), lambda i: (i // per, 0)),
                 resident((d, d)),
                 resident((1, d)), resident(w_up.shape), resident(conv_w.shape), resident(cb.shape),
                 resident(w_down.shape)]
    return pl.pallas_call(
        functools.partial(_post_kernel, n_act=len(acts), tm=tm, seq=seq, nj=nj, tff=tff, dh=dh),
        grid=(t // tm,), in_specs=in_specs, out_specs=tok(d),
        out_shape=jax.ShapeDtypeStruct((t, d), F32),
        scratch_shapes=[pltpu.VMEM((tm, dff), BF16),
                        pltpu.VMEM((2 * nj, SUBLANES, tff), F32)],
        compiler_params=_cparams(("arbitrary",)), name="post",
    )(x, *acts, *wouts, g_xa.reshape(1, d), wq, gq.reshape(1, dh), kmem, vmem, wo,
      g_ffn.reshape(1, d), w_up, conv_w, cb, w_down)


def _pad_cols(w, n):
    return jnp.pad(w, ((0, 0), (0, n - w.shape[1])))


def _mixer_ab(x, g, w_in, conv_w, a_log, dt_bias, dn_g, sb_gq, sb_gk, w_out, batch, seq):
    dk = dn_g.shape[0]
    wq = DN_HEADS * dk
    dh = sb_gq.shape[0]
    wb = SB_HEADS * dh
    c_ab = 4 * wq
    w = jnp.concatenate([w_in[:, :c_ab], _pad_cols(w_in[:, c_ab:c_ab + 2 * DN_HEADS], LANES),
                         w_in[:, c_ab + 2 * DN_HEADS:]], axis=1).astype(BF16)
    o_qb = c_ab + LANES
    plan = [(0, 3 * wq, "raw", 0), (3 * wq, wq, "raw", 0), (c_ab, LANES, "raw", 0),
            (o_qb, wb, "headnorm", dh), (o_qb + wb, wb, "headnorm", dh), (o_qb + 2 * wb, wb, "raw", 0)]
    outs = [(3 * wq, F32), (wq, F32), (LANES, F32), (wb, BF16), (wb, BF16), (wb, BF16)]
    qkv_a, z_a, ab, q_b, k_b, v_b = _proj(
        x, g, w, [sb_gq.reshape(1, dh), sb_gk.reshape(1, dh)], plan, outs, tm=TM_PROJ)

    alog_row = _pad_cols(a_log.reshape(1, -1), LANES)
    dtb_row = _pad_cols(dt_bias.reshape(1, -1), LANES)
    u, wmat, qg, qk, kdt, egl = _dn_prep(qkv_a, ab, conv_w, alog_row, dtb_row, seq, lp=T_DN_PREP)
    nb = next(n for n in (8, 4, 2, 1) if batch % n == 0)
    o_a = _dn_scan(u, wmat, qg, qk, kdt, egl, z_a, dn_g.reshape(1, dk), batch, seq, ls=T_DN_SCAN // nb, nb=nb)
    o_b = _sb_attention(q_b, k_b, v_b, batch, seq, tq=T_SB)
    wo = w_out.astype(BF16)
    return [o_a, o_b], [wo[:wq], wo[wq:]]


def _mixer_gla(x, g, w_in, w_gk, b_gk, norm_g, w_out, batch, seq):
    rank, wk = w_gk.shape
    dv = norm_g.shape[0]
    wv = GLA_HEADS * dv
    w = _pad_cols(w_in, 2 * wk + 2 * wv + LANES).astype(BF16)
    wgk = jnp.pad(w_gk, ((0, LANES - rank), (0, 0))).astype(BF16)
    plan = [(0, wk, "raw", 0), (wk, wk, "raw", 0), (2 * wk, wv, "raw", 0),
            (2 * wk + wv, wv, "raw", 0), (2 * wk + 2 * wv, LANES, "gk", 0)]
    outs = [(wk, F32), (wk, F32), (wv, BF16), (wv, F32), (wk, F32)]
    q, k, v, r, gk = _proj(x, g, w, [wgk, b_gk.reshape(1, wk)], plan, outs, tm=TM_PROJ)
    o = _gla(q, k, gk, v, r, norm_g.reshape(1, dv), batch, seq, ls=T_GLA)
    return [o], [w_out.astype(BF16)]


def _mem_kv(mem, g, w_kv, g_k):
    d = mem.shape[1]
    dh = g_k.shape[0]
    plan = [(0, d, "headnorm", dh), (d, d, "raw", 0)]
    outs = [(d, BF16), (d, BF16)]
    return _proj(mem, g, w_kv.astype(BF16), [g_k.reshape(1, dh)], plan, outs, tm=TM_MEM)


def kernel(x, mem, norm_mix, norm_xa, norm_mem, norm_ffn, xa_w_q, xa_w_kv, xa_w_o, xa_g_q, xa_g_k, ffn_w_up, ffn_conv_w, ffn_conv_b, ffn_w_down, ab_w_in, ab_conv_w, dn_a_log, dn_dt_bias, dn_norm_g, sb_g_q, sb_g_k, ab_w_out, gla_w_in, gla_w_gk, gla_b_gk, gla_norm_g, gla_w_out):
    batch, seq, d = x.shape
    depth = norm_mix.shape[0]
    xf = x.reshape(batch * seq, d)
    memf = mem.reshape(batch * mem.shape[1], d)
    for layer in range(depth):
        i = layer // 2
        if layer % 2 == 0:
            acts, wouts = _mixer_ab(xf, norm_mix[layer], ab_w_in[i], ab_conv_w[i], dn_a_log[i], dn_dt_bias[i],
                                    dn_norm_g[i], sb_g_q[i], sb_g_k[i], ab_w_out[i], batch, seq)
        else:
            acts, wouts = _mixer_gla(xf, norm_mix[layer], gla_w_in[i], gla_w_gk[i], gla_b_gk[i],
                                     gla_norm_g[i], gla_w_out[i], batch, seq)
        kmem, vmem = _mem_kv(memf, norm_mem[layer], xa_w_kv[layer], xa_g_k[layer])
        xf = _post(xf, acts, wouts, norm_xa[layer], xa_w_q[layer].astype(BF16), xa_g_q[layer], kmem, vmem,
                   xa_w_o[layer].astype(BF16), norm_ffn[layer], ffn_w_up[layer].astype(BF16),
                   ffn_conv_w[layer], ffn_conv_b[layer], ffn_w_down[layer].astype(BF16), seq, tm=TM_PROJ, tff=T_FF)
    return xf.reshape(batch, seq, d)
```

```python
import functools

import jax
import jax.numpy as jnp
from jax import lax
from jax.experimental import pallas as pl
from jax.experimental.pallas import tpu as pltpu

F32 = jnp.float32
BF16 = jnp.bfloat16
EPS = 1e-6

DN_HEADS = 4
SB_HEADS = 4
GLA_HEADS = 4
XA_HEADS = 4
GLA_TAU = 16.0
CHUNK = 64
LANES = 128
SUBLANES = 8
VMEM_LIMIT = 56 * 1024 * 1024
SB_UNDERFLOW = 110.0
SB_ROW_BLOCKS = 2
SB_HEADS_PER_STEP = 2
TM_PROJ = 512
TM_MEM = 256
T_DN_PREP = 512
T_DN_SCAN = 1024
T_GLA = 256
T_SB = 256
T_FF = 256


def _cparams(sem):
    return pltpu.CompilerParams(dimension_semantics=sem, vmem_limit_bytes=VMEM_LIMIT)


def _rms(x, g):
    ms = jnp.mean(x * x, axis=-1, keepdims=True)
    return x * lax.rsqrt(ms + EPS) * g


def _l2n(x):
    return x * lax.rsqrt(jnp.sum(x * x, axis=-1, keepdims=True) + EPS)


def _dot(a, b):
    return jnp.dot(a, b, preferred_element_type=F32)


def _dot_nt(a, b):
    return lax.dot_general(a, b, (((1,), (1,)), ((), ())), preferred_element_type=F32)


def _split2(x):
    hi = x.astype(BF16)
    lo = (x - hi.astype(F32)).astype(BF16)
    return hi, lo


def _split3(x):
    hi = x.astype(BF16)
    r = x - hi.astype(F32)
    mid = r.astype(BF16)
    lo = (r - mid.astype(F32)).astype(BF16)
    return hi, mid, lo


def _sigmoid(x):
    return 1.0 / (1.0 + jnp.exp(-x))


def _silu(x):
    return x * _sigmoid(x)


def _softplus(x):
    return jnp.maximum(x, 0.0) + jnp.log1p(jnp.exp(-jnp.abs(x)))


def _chunk_tri(n):
    r = lax.broadcasted_iota(jnp.int32, (n, n), 0)
    c = lax.broadcasted_iota(jnp.int32, (n, n), 1)
    same = (r // CHUNK) == (c // CHUNK)
    return jnp.where(same & (c <= r), 1.0, 0.0).astype(BF16)


def _proj_kernel(*refs, plan, n_aux):
    x_ref, g_ref, w_ref = refs[:3]
    aux = refs[3:3 + n_aux]
    outs = refs[3 + n_aux:]
    hn = _rms(x_ref[...], g_ref[...]).astype(BF16)
    ai = 0
    for (c0, width, mode, hd), o_ref in zip(plan, outs):
        if mode == "headnorm":
            gain = aux[ai][...]
            ai += 1
        elif mode == "gk":
            wgk_ref, bgk_ref = aux[ai], aux[ai + 1]
            ai += 2
        step = min(width, 512)
        for s0 in range(0, width, step):
            p = _dot(hn, w_ref[:, c0 + s0:c0 + s0 + step])
            if mode == "raw":
                o_ref[:, s0:s0 + step] = p.astype(o_ref.dtype)
            elif mode == "headnorm":
                for h0 in range(0, step, hd):
                    o_ref[:, s0 + h0:s0 + h0 + hd] = _rms(p[:, h0:h0 + hd], gain).astype(o_ref.dtype)
            else:
                y = _dot(p.astype(BF16), wgk_ref[...]) + bgk_ref[...]
                o_ref[...] = (jnp.minimum(y, 0.0) - jnp.log1p(jnp.exp(-jnp.abs(y)))) / GLA_TAU


def _proj(x, g, w, aux, plan, out_defs, tm):
    t, d = x.shape
    n = w.shape[1]
    in_specs = [pl.BlockSpec((tm, d), lambda i: (i, 0)),
                pl.BlockSpec((1, d), lambda i: (0, 0)),
                pl.BlockSpec((d, n), lambda i: (0, 0))]
    in_specs += [pl.BlockSpec(a.shape, lambda i: (0, 0)) for a in aux]
    out_shape = [jax.ShapeDtypeStruct((t, wd), dt) for wd, dt in out_defs]
    out_specs = [pl.BlockSpec((tm, wd), lambda i: (i, 0)) for wd, _ in out_defs]
    return pl.pallas_call(
        functools.partial(_proj_kernel, plan=tuple(plan), n_aux=len(aux)),
        grid=(t // tm,), in_specs=in_specs, out_specs=out_specs, out_shape=out_shape,
        compiler_params=_cparams(("parallel",)), name="proj",
    )(x, g.reshape(1, d), w, *aux)


def _dn_prep_kernel(qkv_ref, halo_ref, ab_ref, cw_ref, alog_ref, dtb_ref,
                    u_ref, w_ref, qg_ref, qk_ref, kdt_ref, egl_ref, buf_ref, *, lp, seq, dk):
    nh = DN_HEADS
    wq = nh * dk
    first = (pl.program_id(0) * lp) % seq == 0
    buf_ref[0:SUBLANES, :] = jnp.where(first, 0.0, halo_ref[...])
    buf_ref[SUBLANES:SUBLANES + lp, :] = qkv_ref[...]

    ab = ab_ref[...]
    g = -jnp.exp(alog_ref[...]) * _softplus(ab + dtb_ref[...])
    beta = _sigmoid(ab)
    tri = _chunk_tri(lp)
    g1, g2, g3 = _split3(g)
    gc = _dot(tri, g1) + _dot(tri, g2) + _dot(tri, g3)
    gct = gc.T
    nchunk = lp // CHUNK
    gl_rows = jnp.concatenate(
        [jnp.broadcast_to(gc[(c + 1) * CHUNK - 1:(c + 1) * CHUNK, :], (CHUNK, LANES))
         for c in range(nchunk)], axis=0)
    for c in range(nchunk):
        egl_ref[c] = jnp.exp(gc[(c + 1) * CHUNK - 1:(c + 1) * CHUNK, :])

    cw = cw_ref[...]
    kw = cw.shape[0]

    def convsilu(c0):
        cur = buf_ref[SUBLANES:SUBLANES + lp, c0:c0 + dk]
        halo = buf_ref[0:SUBLANES, c0:c0 + dk]
        y = cw[kw - 1:kw, c0:c0 + dk] * cur
        row = lax.broadcasted_iota(jnp.int32, (SUBLANES, dk), 0)
        for s in range(1, kw):
            shifted = pltpu.roll(cur, s, axis=0)
            head = jnp.where(row < s, pltpu.roll(halo, s, axis=0), shifted[:SUBLANES])
            shifted = jnp.concatenate([head, shifted[SUBLANES:]], axis=0)
            y = y + cw[kw - 1 - s:kw - s, c0:c0 + dk] * shifted
        return _silu(y)

    pb = 2 * CHUNK
    pairs = range(lp // pb)
    ii = lax.broadcasted_iota(jnp.int32, (pb, pb), 0)
    jj = lax.broadcasted_iota(jnp.int32, (pb, pb), 1)
    same = (ii // CHUNK) == (jj // CHUNK)
    causal = same & (jj <= ii)
    strict = same & (jj < ii)
    eye = jnp.where(ii == jj, 1.0, 0.0)

    a_mats, vbs, kbgs = [], [], []
    for h in range(nh):
        cs = slice(h * dk, (h + 1) * dk)
        qn = _l2n(convsilu(h * dk)) * (dk ** -0.5)
        kn = _l2n(convsilu(wq + h * dk))
        v = convsilu(2 * wq + h * dk)
        gcol = gc[:, h:h + 1]
        bcol = beta[:, nh + h:nh + h + 1]
        egc = jnp.exp(gcol)
        kb = kn * bcol
        vbs.append(v * bcol)
        kbgs.append((kb * egc).astype(BF16))
        qg_ref[:, cs] = (qn * egc).astype(BF16)
        kdec = kn * jnp.exp(gl_rows[:, h:h + 1] - gcol)
        kdt = kdec.T.astype(BF16)
        for c in range(nchunk):
            kdt_ref[c, :, h * CHUNK:(h + 1) * CHUNK] = kdt[:, c * CHUNK:(c + 1) * CHUNK]
        knb = kn.astype(BF16)
        kbb = kb.astype(BF16)
        qnb = qn.astype(BF16)
        for p in pairs:
            pr = slice(p * pb, (p + 1) * pb)
            diff = gcol[pr] - gct[h:h + 1, pr]
            decay = jnp.where(causal, jnp.exp(jnp.where(causal, diff, 0.0)), 0.0)
            a_mats.append(jnp.where(strict, _dot_nt(kbb[pr], knb[pr]) * decay, 0.0))
            qk = (_dot_nt(qnb[pr], knb[pr]) * decay).astype(BF16)
            for half in range(2):
                sub = slice(half * CHUNK, (half + 1) * CHUNK)
                r0 = p * pb + half * CHUNK
                qk_ref[r0:r0 + CHUNK, h * CHUNK:(h + 1) * CHUNK] = qk[sub, sub]

    x_invs = [eye - a for a in a_mats]
    ps = a_mats
    for _ in range(5):
        pbs = [p.astype(BF16) for p in ps]
        ps = [_dot(b, b) for b in pbs]
        x_invs = [x + _dot(x.astype(BF16), p.astype(BF16)) for x, p in zip(x_invs, ps)]
    for h in range(nh):
        cs = slice(h * dk, (h + 1) * dk)
        for p in pairs:
            pr = slice(p * pb, (p + 1) * pb)
            x1, x2 = _split2(x_invs[h * len(pairs) + p])
            v1, v2 = _split2(vbs[h][pr])
            u_ref[pr, cs] = _dot(x1, v1) + _dot(x1, v2) + _dot(x2, v1)
            w_ref[pr, cs] = _dot(x1, kbgs[h][pr]).astype(BF16)


def _dn_prep(qkv, ab, conv_w, alog_row, dtb_row, seq, lp):
    t, wqkv = qkv.shape
    dk = wqkv // (3 * DN_HEADS)
    wq = DN_HEADS * dk
    hb = lp // SUBLANES
    nchunk = lp // CHUNK
    out_shape = [jax.ShapeDtypeStruct((t, wq), F32),
                 jax.ShapeDtypeStruct((t, wq), BF16),
                 jax.ShapeDtypeStruct((t, wq), BF16),
                 jax.ShapeDtypeStruct((t, DN_HEADS * CHUNK), BF16),
                 jax.ShapeDtypeStruct((t // CHUNK, dk, DN_HEADS * CHUNK), BF16),
                 jax.ShapeDtypeStruct((t // CHUNK, 1, LANES), F32)]
    out_specs = [pl.BlockSpec((lp, wq), lambda i: (i, 0)),
                 pl.BlockSpec((lp, wq), lambda i: (i, 0)),
                 pl.BlockSpec((lp, wq), lambda i: (i, 0)),
                 pl.BlockSpec((lp, DN_HEADS * CHUNK), lambda i: (i, 0)),
                 pl.BlockSpec((nchunk, dk, DN_HEADS * CHUNK), lambda i: (i, 0, 0)),
                 pl.BlockSpec((nchunk, 1, LANES), lambda i: (i, 0, 0))]
    in_specs = [pl.BlockSpec((lp, wqkv), lambda i: (i, 0)),
                pl.BlockSpec((SUBLANES, wqkv), lambda i: (jnp.maximum(i * hb - 1, 0), 0)),
                pl.BlockSpec((lp, LANES), lambda i: (i, 0)),
                pl.BlockSpec(conv_w.shape, lambda i: (0, 0)),
                pl.BlockSpec((1, LANES), lambda i: (0, 0)),
                pl.BlockSpec((1, LANES), lambda i: (0, 0))]
    return pl.pallas_call(
        functools.partial(_dn_prep_kernel, lp=lp, seq=seq, dk=dk),
        grid=(t // lp,), in_specs=in_specs, out_specs=out_specs, out_shape=out_shape,
        scratch_shapes=[pltpu.VMEM((lp + SUBLANES, wqkv), F32)],
        compiler_params=_cparams(("parallel",)), name="dn_prep",
    )(qkv, qkv, ab, conv_w, alog_row, dtb_row)


def _dn_scan_kernel(u_ref, w_ref, qg_ref, qk_ref, kdt_ref, egl_ref, z_ref, gain_ref,
                    o_ref, s_ref, *, ls, dk, nb):
    @pl.when(pl.program_id(1) == 0)
    def _():
        s_ref[...] = jnp.zeros_like(s_ref)

    gain = gain_ref[...]
    heads = range(DN_HEADS)
    cs = [slice(h * dk, (h + 1) * dk) for h in heads]
    hs = [slice(h * CHUNK, (h + 1) * CHUNK) for h in heads]
    items = [(b, h) for b in range(nb) for h in heads]

    def body(c, carry):
        r0 = pl.multiple_of(c * CHUNK, CHUNK)
        rows = pl.ds(r0, CHUNK)
        e_full = [jnp.broadcast_to(egl_ref[b, c], (dk, LANES)) for b in range(nb)]
        ss = [s_ref[b * DN_HEADS + h] for b, h in items]
        sbs = [s.astype(BF16) for s in ss]
        wss = [_dot(w_ref[b, rows, cs[h]], sbs[i]) for i, (b, h) in enumerate(items)]
        oqs = [_dot(qg_ref[b, rows, cs[h]], sbs[i]) for i, (b, h) in enumerate(items)]
        dbs = [(u_ref[b, rows, cs[h]] - wss[i]).astype(BF16) for i, (b, h) in enumerate(items)]
        ods = [_dot(qk_ref[b, rows, hs[h]], dbs[i]) for i, (b, h) in enumerate(items)]
        sds = [_dot(kdt_ref[b, c, :, hs[h]], dbs[i]) for i, (b, h) in enumerate(items)]
        for i, (b, h) in enumerate(items):
            s_ref[b * DN_HEADS + h] = e_full[b][:, h:h + 1] * ss[i] + sds[i]
        for i, (b, h) in enumerate(items):
            o = oqs[i] + ods[i]
            o_ref[b, rows, cs[h]] = (_rms(o, gain) * _silu(z_ref[b, rows, cs[h]])).astype(BF16)
        return carry

    lax.fori_loop(0, ls // CHUNK, body, 0)


def _dn_scan(u, w, qg, qk, kdt, egl, z, gain, batch, seq, ls, nb):
    t, wq = u.shape
    dk = wq // DN_HEADS
    ns = seq // ls
    nchunk = ls // CHUNK
    seq3 = lambda a: a.reshape(batch, seq, a.shape[-1])
    tok = lambda wd: pl.BlockSpec((nb, ls, wd), lambda b, s: (b, s, 0))
    in_specs = [tok(wq), tok(wq), tok(wq), tok(DN_HEADS * CHUNK),
                pl.BlockSpec((nb, nchunk, dk, DN_HEADS * CHUNK), lambda b, s: (b, s, 0, 0)),
                pl.BlockSpec((nb, nchunk, 1, LANES), lambda b, s: (b, s, 0, 0)),
                tok(wq),
                pl.BlockSpec((1, dk), lambda b, s: (0, 0))]
    out = pl.pallas_call(
        functools.partial(_dn_scan_kernel, ls=ls, dk=dk, nb=nb),
        grid=(batch // nb, ns), in_specs=in_specs, out_specs=tok(wq),
        out_shape=jax.ShapeDtypeStruct((batch, seq, wq), BF16),
        scratch_shapes=[pltpu.VMEM((nb * DN_HEADS, dk, dk), F32)],
        compiler_params=_cparams(("parallel", "arbitrary")), name="dn_scan",
    )(seq3(u), seq3(w), seq3(qg), seq3(qk),
      kdt.reshape(batch, seq // CHUNK, dk, DN_HEADS * CHUNK), egl.reshape(batch, seq // CHUNK, 1, LANES),
      seq3(z), gain)
    return out.reshape(t, wq)


def _sb_kernel(q_ref, k_ref, v_ref, o_ref, acc_ref, r_ref, kmax_ref, *, tq, nblk, nh, dh, scale):
    qi = pl.program_id(2)
    rb = tq // nblk
    heads = range(nh)
    hc = [slice(h * dh, (h + 1) * dh) for h in heads]

    @pl.when(qi == 0)
    def _():
        for h in heads:
            kf = k_ref[:, hc[h]].astype(F32)
            kn2 = jnp.sum(kf * kf, axis=-1, keepdims=True)
            kmax_ref[h:h + 1, :] = jnp.broadcast_to(jnp.sqrt(jnp.max(kn2, axis=0, keepdims=True)), (1, LANES))

    zb = []
    for h in heads:
        qf = q_ref[:, hc[h]].astype(F32)
        zb.append(scale * jnp.sqrt(jnp.sum(qf * qf, axis=-1, keepdims=True)) * kmax_ref[h:h + 1, 0:1])
    rr = lax.broadcasted_iota(jnp.int32, (tq, tq), 0)
    cc = lax.broadcasted_iota(jnp.int32, (tq, tq), 1)
    suffix = jnp.where(rr >= cc, 1.0, 0.0).astype(BF16)

    acc_ref[...] = jnp.zeros_like(acc_ref)
    r_ref[...] = jnp.zeros_like(r_ref)

    blocks = range(nblk)
    rs = [slice(b * rb, (b + 1) * rb) for b in blocks]
    kpos = lax.broadcasted_iota(jnp.int32, (rb, tq), 1)
    strict = [kpos < lax.broadcasted_iota(jnp.int32, (rb, tq), 0) + b * rb for b in blocks]

    def tiles(specs):
        tix = range(len(specs))
        items = [(h, t, b) for h in heads for t in tix for b in blocks]
        k0 = [pl.multiple_of(kt * tq, tq) for kt, _, _ in specs]
        zs = {(h, t, b): _dot_nt(q_ref[rs[b], hc[h]], k_ref[pl.ds(k0[t], tq), hc[h]]) * scale
              for h, t, b in items}
        ls = {it: -(jnp.maximum(z, 0.0) + jnp.log(1.0 + jnp.exp(-jnp.abs(z)))) for it, z in zs.items()}
        for h, t, b in items:
            if specs[t][1]:
                ls[h, t, b] = jnp.where(strict[b], ls[h, t, b], 0.0)
        parts = {it: _split2(l) for it, l in ls.items()}
        csums = {it: _dot(hi, suffix) + _dot(lo, suffix) for it, (hi, lo) in parts.items()}
        carry = {(h, b): r_ref[h, rs[b], :] for h in heads for b in blocks}
        atts = {}
        for h, t, b in items:
            att = jnp.exp(zs[h, t, b] + csums[h, t, b] + carry[h, b])
            if specs[t][1]:
                att = jnp.where(strict[b], att, 0.0)
            if specs[t][2] is not None:
                att = jnp.where(specs[t][2], att, 0.0)
            atts[h, t, b] = att.astype(BF16)
            carry[h, b] = carry[h, b] + csums[h, t, b][:, 0:1]
        pvs = {(h, t, b): _dot(atts[h, t, b], v_ref[pl.ds(k0[t], tq), hc[h]]) for h, t, b in items}
        for h in heads:
            for b in blocks:
                acc = acc_ref[rs[b], hc[h]]
                for t in tix:
                    acc = acc + pvs[h, t, b]
                acc_ref[rs[b], hc[h]] = acc
                r_ref[h, rs[b], :] = carry[h, b]

    def live():
        m = jnp.max(r_ref[0] + zb[0])
        for h in range(1, nh):
            m = jnp.maximum(m, jnp.max(r_ref[h] + zb[h]))
        return m

    tiles([(qi, True, None), (jnp.maximum(qi - 1, 0), False, qi > 0)])

    def cond(carry):
        j, m = carry
        return jnp.logical_and(j < qi, m > -SB_UNDERFLOW)

    def body(carry):
        j, _ = carry
        tiles([(qi - 1 - j, False, None)])
        return j + 1, live()

    lax.while_loop(cond, body, (jnp.int32(1), live()))
    o_ref[...] = acc_ref[...].astype(BF16)


def _sb_attention(q, k, v, batch, seq, tq):
    t, wq = q.shape
    dh = wq // SB_HEADS
    nq = seq // tq
    nh = SB_HEADS_PER_STEP
    wh = nh * dh
    return pl.pallas_call(
        functools.partial(_sb_kernel, tq=tq, nblk=SB_ROW_BLOCKS, nh=nh, dh=dh, scale=dh ** -0.5),
        grid=(batch, SB_HEADS // nh, nq),
        in_specs=[pl.BlockSpec((tq, wh), lambda b, h, i: (b * nq + i, h)),
                  pl.BlockSpec((seq, wh), lambda b, h, i: (b, h)),
                  pl.BlockSpec((seq, wh), lambda b, h, i: (b, h))],
        out_specs=pl.BlockSpec((tq, wh), lambda b, h, i: (b * nq + i, h)),
        out_shape=jax.ShapeDtypeStruct((t, wq), BF16),
        scratch_shapes=[pltpu.VMEM((tq, wh), F32), pltpu.VMEM((nh, tq, 1), F32),
                        pltpu.VMEM((SUBLANES, LANES), F32)],
        compiler_params=_cparams(("parallel", "parallel", "arbitrary")), name="sb_attn",
    )(q, k, v)


def _gla_kernel(q_ref, k_ref, gk_ref, v_ref, r_ref, gain_ref, o_ref, s_ref, *, ls, dk, dv):
    @pl.when(pl.program_id(1) == 0)
    def _():
        s_ref[...] = jnp.zeros_like(s_ref)

    gain = gain_ref[...]
    tri = _chunk_tri(ls)
    g1, g2, g3 = _split3(gk_ref[...])
    bc_all = _dot(tri, g1) + _dot(tri, g2) + _dot(tri, g3)
    ii = lax.broadcasted_iota(jnp.int32, (ls, ls), 0)
    jj = lax.broadcasted_iota(jnp.int32, (ls, ls), 1)
    causal = ((ii // CHUNK) == (jj // CHUNK)) & (jj <= ii)
    chunks = range(ls // CHUNK)
    heads = range(GLA_HEADS)
    rs = [slice(c * CHUNK, (c + 1) * CHUNK) for c in chunks]
    ks = [slice(h * dk, (h + 1) * dk) for h in heads]
    vs = [slice(h * dv, (h + 1) * dv) for h in heads]

    def chunk_row(x, r):
        return jnp.concatenate(
            [jnp.broadcast_to(x[c * CHUNK + r:c * CHUNK + r + 1, :], (CHUNK, x.shape[1])) for c in chunks], axis=0)

    bcs = [bc_all[:, ks[h]] for h in heads]
    bmids = [chunk_row(bc, CHUNK // 2) for bc in bcs]
    blasts = [chunk_row(bc, CHUNK - 1) for bc in bcs]
    qs = [q_ref[:, ks[h]] * (dk ** -0.5) for h in heads]
    kk = [k_ref[:, ks[h]] for h in heads]
    vv = [v_ref[:, vs[h]] for h in heads]
    atts = [_dot_nt((qs[h] * jnp.exp(bcs[h] - bmids[h])).astype(BF16),
                    (kk[h] * jnp.exp(bmids[h] - bcs[h])).astype(BF16)) for h in heads]
    atts = [jnp.where(causal, a, 0.0).astype(BF16) for a in atts]
    o_intra = [_dot(atts[h], vv[h]) for h in heads]
    kdts = [(kk[h] * jnp.exp(blasts[h] - bcs[h])).T.astype(BF16) for h in heads]
    bcts = [bc.T for bc in bcs]
    qgs = [(qs[h] * jnp.exp(bcs[h])).astype(BF16) for h in heads]
    upd = [[_dot(kdts[h][:, rs[c]], vv[h][rs[c]]) for c in chunks] for h in heads]
    snaps = []
    for h in heads:
        s = s_ref[h]
        sn = []
        for c in chunks:
            sn.append(s.astype(BF16))
            last = (c + 1) * CHUNK - 1
            s = jnp.exp(bcts[h][:, last:last + 1]) * s + upd[h][c]
        s_ref[h] = s
        snaps.append(sn)
    for h in heads:
        o_inter = jnp.concatenate([_dot(qgs[h][rs[c]], snaps[h][c]) for c in chunks], axis=0)
        o = o_intra[h] + o_inter
        o_ref[:, vs[h]] = (_rms(o, gain) * _silu(r_ref[:, vs[h]])).astype(BF16)


def _gla(q, k, gk, v, r, gain, batch, seq, ls):
    t, wk = q.shape
    wv = v.shape[1]
    dk = wk // GLA_HEADS
    dv = wv // GLA_HEADS
    ns = seq // ls
    tok = lambda wd: pl.BlockSpec((ls, wd), lambda b, s: (b * ns + s, 0))
    return pl.pallas_call(
        functools.partial(_gla_kernel, ls=ls, dk=dk, dv=dv),
        grid=(batch, ns),
        in_specs=[tok(wk), tok(wk), tok(wk), tok(wv), tok(wv),
                  pl.BlockSpec((1, dv), lambda b, s: (0, 0))],
        out_specs=tok(wv),
        out_shape=jax.ShapeDtypeStruct((t, wv), BF16),
        scratch_shapes=[pltpu.VMEM((GLA_HEADS, dk, dv), F32)],
        compiler_params=_cparams(("parallel", "arbitrary")), name="gla",
    )(q, k, gk, v, r, gain)


def _post_kernel(*refs, n_act, tm, seq, nj, tff, dh):
    x_ref = refs[0]
    a_refs = refs[1:1 + n_act]
    wout_refs = refs[1 + n_act:1 + 2 * n_act]
    (gxa_ref, wq_ref, gq_ref, k_ref, v_ref, wo_ref,
     gff_ref, wup_ref, cw_ref, cb_ref, wd_ref, o_ref, act_ref, carry_ref) = refs[1 + 2 * n_act:]

    x = x_ref[...]
    for a_ref, w_ref in zip(a_refs, wout_refs):
        x = x + _dot(a_ref[...], w_ref[...])

    hn = _rms(x, gxa_ref[...]).astype(BF16)
    gq = gq_ref[...]
    heads = range(XA_HEADS)
    hs = [slice(h * dh, (h + 1) * dh) for h in heads]
    qs = [_dot(hn, wq_ref[:, hs[h]]) for h in heads]
    qn = [_rms(q, gq).astype(BF16) for q in qs]
    sc = [_dot_nt(qn[h], k_ref[:, hs[h]]) * (dh ** -0.5) for h in heads]
    es = [jnp.exp(s - jnp.max(s, axis=-1, keepdims=True)) for s in sc]
    pr = [(e / jnp.sum(e, axis=-1, keepdims=True)).astype(BF16) for e in es]
    oh = [_dot(pr[h], v_ref[:, hs[h]]).astype(BF16) for h in heads]
    x = x + _dot(jnp.concatenate(oh, axis=1), wo_ref[...])

    hn = _rms(x, gff_ref[...]).astype(BF16)
    first = (pl.program_id(0) * tm) % seq == 0
    kw = cw_ref.shape[0]

    def branch(idx):
        cols = slice(idx * tff, (idx + 1) * tff)
        u = _dot(hn, wup_ref[:, cols])
        halo = jnp.where(first, 0.0, carry_ref[idx])
        carry_ref[idx] = u[tm - SUBLANES:tm, :]
        ext = jnp.concatenate([halo, u], axis=0)
        cw = cw_ref[:, cols]
        y = cb_ref[:, cols] + cw[kw - 1:kw, :] * u
        for s in range(1, kw):
            y = y + cw[kw - 1 - s:kw - s, :] * ext[SUBLANES - s:SUBLANES - s + tm, :]
        return y

    for j in range(nj):
        gate = branch(j)
        val = branch(nj + j)
        act_ref[:, j * tff:(j + 1) * tff] = (_silu(gate) * val).astype(BF16)
    o_ref[...] = x + _dot(act_ref[...], wd_ref[...])


def _post(x, acts, wouts, g_xa, wq, gq, kmem, vmem, wo, g_ffn, w_up, conv_w, conv_b, w_down, seq, tm, tff):
    t, d = x.shape
    dh = gq.shape[0]
    nm = kmem.shape[0] // (t // seq)
    per = seq // tm
    dff = w_down.shape[0]
    nj = dff // tff
    cb = conv_b.reshape(1, -1)
    resident = lambda shape: pl.BlockSpec(shape, lambda i: (0,) * len(shape), pipeline_mode=pl.Buffered(1))
    tok = lambda wd: pl.BlockSpec((tm, wd), lambda i: (i, 0))
    in_specs = [tok(d)] + [tok(a.shape[1]) for a in acts] + [resident(w.shape) for w in wouts]
    in_specs += [resident((1, d)), resident((d, d)), resident((1, dh)),
                 pl.BlockSpec((nm, d), lambda i: (i // per, 0)),
                 pl.BlockSpec((nm, d), lambda i: (i // per, 0)),
                 resident((d, d)),
                 resident((1, d)), resident(w_up.shape), resident(conv_w.shape), resident(cb.shape),
                 resident(w_down.shape)]
    return pl.pallas_call(
        functools.partial(_post_kernel, n_act=len(acts), tm=tm, seq=seq, nj=nj, tff=tff, dh=dh),
        grid=(t // tm,), in_specs=in_specs, out_specs=tok(d),
        out_shape=jax.ShapeDtypeStruct((t, d), F32),
        scratch_shapes=[pltpu.VMEM((tm, dff), BF16),
                        pltpu.VMEM((2 * nj, SUBLANES, tff), F32)],
        compiler_params=_cparams(("arbitrary",)), name="post",
    )(x, *acts, *wouts, g_xa.reshape(1, d), wq, gq.reshape(1, dh), kmem, vmem, wo,
      g_ffn.reshape(1, d), w_up, conv_w, cb, w_down)


def _pad_cols(w, n):
    return jnp.pad(w, ((0, 0), (0, n - w.shape[1])))


def _mixer_ab(x, g, w_in, conv_w, a_log, dt_bias, dn_g, sb_gq, sb_gk, w_out, batch, seq):
    dk = dn_g.shape[0]
    wq = DN_HEADS * dk
    dh = sb_gq.shape[0]
    wb = SB_HEADS * dh
    c_ab = 4 * wq
    w = jnp.concatenate([w_in[:, :c_ab], _pad_cols(w_in[:, c_ab:c_ab + 2 * DN_HEADS], LANES),
                         w_in[:, c_ab + 2 * DN_HEADS:]], axis=1).astype(BF16)
    o_qb = c_ab + LANES
    plan = [(0, 3 * wq, "raw", 0), (3 * wq, wq, "raw", 0), (c_ab, LANES, "raw", 0),
            (o_qb, wb, "headnorm", dh), (o_qb + wb, wb, "headnorm", dh), (o_qb + 2 * wb, wb, "raw", 0)]
    outs = [(3 * wq, F32), (wq, F32), (LANES, F32), (wb, BF16), (wb, BF16), (wb, BF16)]
    qkv_a, z_a, ab, q_b, k_b, v_b = _proj(
        x, g, w, [sb_gq.reshape(1, dh), sb_gk.reshape(1, dh)], plan, outs, tm=TM_PROJ)

    alog_row = _pad_cols(a_log.reshape(1, -1), LANES)
    dtb_row = _pad_cols(dt_bias.reshape(1, -1), LANES)
    u, wmat, qg, qk, kdt, egl = _dn_prep(qkv_a, ab, conv_w, alog_row, dtb_row, seq, lp=T_DN_PREP)
    nb = next(n for n in (8, 4, 2, 1) if batch % n == 0)
    o_a = _dn_scan(u, wmat, qg, qk, kdt, egl, z_a, dn_g.reshape(1, dk), batch, seq, ls=T_DN_SCAN // nb, nb=nb)
    o_b = _sb_attention(q_b, k_b, v_b, batch, seq, tq=T_SB)
    wo = w_out.astype(BF16)
    return [o_a, o_b], [wo[:wq], wo[wq:]]


def _mixer_gla(x, g, w_in, w_gk, b_gk, norm_g, w_out, batch, seq):
    rank, wk = w_gk.shape
    dv = norm_g.shape[0]
    wv = GLA_HEADS * dv
    w = _pad_cols(w_in, 2 * wk + 2 * wv + LANES).astype(BF16)
    wgk = jnp.pad(w_gk, ((0, LANES - rank), (0, 0))).astype(BF16)
    plan = [(0, wk, "raw", 0), (wk, wk, "raw", 0), (2 * wk, wv, "raw", 0),
            (2 * wk + wv, wv, "raw", 0), (2 * wk + 2 * wv, LANES, "gk", 0)]
    outs = [(wk, F32), (wk, F32), (wv, BF16), (wv, F32), (wk, F32)]
    q, k, v, r, gk = _proj(x, g, w, [wgk, b_gk.reshape(1, wk)], plan, outs, tm=TM_PROJ)
    o = _gla(q, k, gk, v, r, norm_g.reshape(1, dv), batch, seq, ls=T_GLA)
    return [o], [w_out.astype(BF16)]


def _mem_kv(mem, g, w_kv, g_k):
    d = mem.shape[1]
    dh = g_k.shape[0]
    plan = [(0, d, "headnorm", dh), (d, d, "raw", 0)]
    outs = [(d, BF16), (d, BF16)]
    return _proj(mem, g, w_kv.astype(BF16), [g_k.reshape(1, dh)], plan, outs, tm=TM_MEM)


def kernel(x, mem, norm_mix, norm_xa, norm_mem, norm_ffn, xa_w_q, xa_w_kv, xa_w_o, xa_g_q, xa_g_k, ffn_w_up, ffn_conv_w, ffn_conv_b, ffn_w_down, ab_w_in, ab_conv_w, dn_a_log, dn_dt_bias, dn_norm_g, sb_g_q, sb_g_k, ab_w_out, gla_w_in, gla_w_gk, gla_b_gk, gla_norm_g, gla_w_out):
    batch, seq, d = x.shape
    depth = norm_mix.shape[0]
    xf = x.reshape(batch * seq, d)
    memf = mem.reshape(batch * mem.shape[1], d)
    for layer in range(depth):
        i = layer // 2
        if layer % 2 == 0:
            acts, wouts = _mixer_ab(xf, norm_mix[layer], ab_w_in[i], ab_conv_w[i], dn_a_log[i], dn_dt_bias[i],
                                    dn_norm_g[i], sb_g_q[i], sb_g_k[i], ab_w_out[i], batch, seq)
        else:
            acts, wouts = _mixer_gla(xf, norm_mix[layer], gla_w_in[i], gla_w_gk[i], gla_b_gk[i],
                                     gla_norm_g[i], gla_w_out[i], batch, seq)
        kmem, vmem = _mem_kv(memf, norm_mem[layer], xa_w_kv[layer], xa_g_k[layer])
        xf = _post(xf, acts, wouts, norm_xa[layer], xa_w_q[layer].astype(BF16), xa_g_q[layer], kmem, vmem,
                   xa_w_o[layer].astype(BF16), norm_ffn[layer], ffn_w_up[layer].astype(BF16),
                   ffn_conv_w[layer], ffn_conv_b[layer], ffn_w_down[layer].astype(BF16), seq, tm=TM_PROJ, tff=T_FF)
    return xf.reshape(batch, seq, d)
```
